```python
import jax, jax.numpy as jnp
from jax import lax
import numpy as np

D_MODEL = 1024
BATCH = 32
SEQ = 256
DEPTH = 2
DEC_BATCH = 4
DEC_SEQ = 4096
PAST_LEN = 256

GRID_W = 64
N_EVEN = (DEPTH + 1) // 2
N_ODD = DEPTH // 2
HEAD_DIM = 64
NA_WIDTH = D_MODEL // 2
NA_HEADS = NA_WIDTH // HEAD_DIM
WIN_ROWS = 8
WIN_COLS = 16
FN_WIDTH = D_MODEL // 2
FN_GROUP_DIM = 64
FN_GROUPS = FN_WIDTH // FN_GROUP_DIM
SG_WIDTH = D_MODEL
SG_GROUPS = 4
SG_GROUP_DIM = SG_WIDTH // SG_GROUPS
CHUNK = 128
N_EXPERTS = 16
N_EXPERT_GROUPS = 4
EXPERTS_PER_GROUP = N_EXPERTS // N_EXPERT_GROUPS
TOP_K = 2
EXPERT_FF = D_MODEL // 4
EPS = 1e-6
NEG_INF = -1e30

kernel_name = "hybrid_natten_fnet_gmlp_moe_diffusion_step"


def rmsnorm(x, g):
    xf = x.astype(jnp.float32)
    y = xf * lax.rsqrt(jnp.mean(xf * xf, axis=-1, keepdims=True) + EPS)
    return (y * g.astype(jnp.float32)).astype(x.dtype)


def modulate(x, g, shift, scale):
    return rmsnorm(x, g) * (1 + scale) + shift


def ada_chunks(cond, w, b):
    m = jax.nn.silu(cond) @ w + b
    return jnp.split(m, 6, axis=-1)


def context_attention(q, k, v):
    s = jnp.einsum('bqhd,bkhd->bhqk', q, k).astype(jnp.float32) * (HEAD_DIM ** -0.5)
    p = jax.nn.softmax(s, axis=-1).astype(v.dtype)
    out = jnp.einsum('bhqk,bkhd->bqhd', p, v)
    return out.reshape(q.shape[0], q.shape[1], NA_WIDTH)


def neighbourhood_attention(q, k, v, k_ctx, v_ctx, rpb):
    B, L, H, Dh = q.shape
    rows = L // GRID_W
    wr = min(WIN_ROWS, rows)
    r = jnp.arange(rows)
    row_start = jnp.clip(r - wr // 2, 0, rows - wr)
    row_idx = row_start[:, None] + jnp.arange(wr)[None, :]
    col = jnp.arange(GRID_W)
    col_start = jnp.clip(col - WIN_COLS // 2, 0, GRID_W - WIN_COLS)
    col_valid = (col[None, :] >= col_start[:, None]) & (col[None, :] < col_start[:, None] + WIN_COLS)
    qg = q.reshape(B, rows, GRID_W, H, Dh)
    kg = k.reshape(B, rows, GRID_W, H, Dh)[:, row_idx]
    vg = v.reshape(B, rows, GRID_W, H, Dh)[:, row_idx]
    scale = HEAD_DIM ** -0.5
    s_lat = jnp.einsum('brqhd,brikhd->brhqik', qg, kg).astype(jnp.float32) * scale
    dr = row_idx - r[:, None] + (WIN_ROWS - 1)
    dc = jnp.clip(col[None, :] - col[:, None] + (WIN_COLS - 1), 0, 2 * WIN_COLS - 2)
    bias = rpb[:, dr[:, None, :, None], dc[None, :, None, :]]
    s_lat = s_lat + jnp.transpose(bias, (1, 0, 2, 3, 4)).astype(jnp.float32)[None]
    s_lat = jnp.where(col_valid[:, None, :], s_lat, NEG_INF).reshape(B, rows, H, GRID_W, wr * GRID_W)
    s_ctx = jnp.einsum('brqhd,bkhd->brhqk', qg, k_ctx).astype(jnp.float32) * scale
    p = jax.nn.softmax(jnp.concatenate([s_lat, s_ctx], axis=-1), axis=-1).astype(v.dtype)
    p_lat = p[..., :wr * GRID_W].reshape(B, rows, H, GRID_W, wr, GRID_W)
    p_ctx = p[..., wr * GRID_W:]
    out = (jnp.einsum('brhqik,brikhd->brqhd', p_lat, vg)
           + jnp.einsum('brhqk,bkhd->brqhd', p_ctx, v_ctx))
    return out.reshape(B, L, NA_WIDTH)


def fourier_mix(u):
    B, L, _ = u.shape
    uf = u.astype(jnp.float32).reshape(B, L, FN_GROUPS, FN_GROUP_DIM)
    y = jnp.fft.fft2(uf, axes=(1, 3), norm='ortho').real
    return y.reshape(B, L, FN_WIDTH).astype(u.dtype)


def spatial_gating(z, v_gain, w_s, b_s):
    B, L, _ = z.shape
    z = jax.nn.gelu(z)
    u, v = jnp.split(z, 2, axis=-1)
    v = rmsnorm(v.reshape(B, L, SG_GROUPS, SG_GROUP_DIM), v_gain.reshape(SG_GROUPS, SG_GROUP_DIM))
    vc = v.reshape(B, L // CHUNK, CHUNK, SG_GROUPS, SG_GROUP_DIM)
    s = jnp.einsum('gpq,bnqgd->bnpgd', w_s, vc) + b_s.T[:, :, None]
    return u * s.reshape(B, L, SG_WIDTH)


def moe(h, router_w, router_b, w_gate, w_up, w_down):
    B, L, D = h.shape
    t = h.reshape(B * L, D)
    scores = jax.nn.sigmoid((t @ router_w).astype(jnp.float32))
    sel = (scores + router_b.astype(jnp.float32)).reshape(-1, N_EXPERT_GROUPS, EXPERTS_PER_GROUP)
    group_score = lax.top_k(sel, TOP_K)[0].sum(-1)
    g_best = jnp.argmax(group_score, axis=-1)
    sel_in = jnp.take_along_axis(sel, g_best[:, None, None], axis=1)[:, 0]
    _, local = lax.top_k(sel_in, TOP_K)
    ids = g_best[:, None] * EXPERTS_PER_GROUP + local
    w = jnp.take_along_axis(scores, ids, axis=1)
    w = w / jnp.sum(w, axis=-1, keepdims=True)
    gates = jnp.sum(jax.nn.one_hot(ids, N_EXPERTS, dtype=jnp.float32) * w[..., None], axis=1).astype(h.dtype)
    hid = jax.nn.silu(jnp.einsum('nd,edf->nef', t, w_gate)) * jnp.einsum('nd,edf->nef', t, w_up)
    y = jnp.einsum('nef,efd->nd', hid * gates[:, :, None], w_down)
    return y.reshape(B, L, D)


def _trunk(x, cond, cache_k, cache_v, w):
    (ada_w, ada_b, norm1_g, norm2_g, na_w_in, na_w_out, q_gain, k_gain, rpb,
     sg_w_in, sg_w_out, sg_v_gain, sg_w_s, sg_b_s, router_w, router_b,
     moe_w_gate, moe_w_up, moe_w_down) = w
    is_context = cache_k is None
    ks, vs = [], []
    B, L, _ = x.shape
    for layer in range(DEPTH):
        sh1, sc1, g1, sh2, sc2, g2 = ada_chunks(cond, ada_w[layer], ada_b[layer])
        h = modulate(x, norm1_g[layer], sh1, sc1)
        if layer % 2 == 0:
            i = layer // 2
            proj = h @ na_w_in[i]
            q, k, v, u = jnp.split(proj, [NA_WIDTH, 2 * NA_WIDTH, 3 * NA_WIDTH], axis=-1)
            q = rmsnorm(q.reshape(B, L, NA_HEADS, HEAD_DIM), q_gain[i])
            k = rmsnorm(k.reshape(B, L, NA_HEADS, HEAD_DIM), k_gain[i])
            v = v.reshape(B, L, NA_HEADS, HEAD_DIM)
            if is_context:
                a = context_attention(q, k, v)
                ks.append(k)
                vs.append(v)
            else:
                a = neighbourhood_attention(q, k, v, cache_k[:, i], cache_v[:, i], rpb[i])
            mix = jnp.concatenate([a, fourier_mix(u)], axis=-1) @ na_w_out[i]
        else:
            j = layer // 2
            mix = spatial_gating(h @ sg_w_in[j], sg_v_gain[j], sg_w_s[j], sg_b_s[j]) @ sg_w_out[j]
        x = x + g1 * mix
        h = modulate(x, norm2_g[layer], sh2, sc2)
        x = x + g2 * moe(h, router_w, router_b, moe_w_gate[layer], moe_w_up[layer], moe_w_down[layer])
    return x, ks, vs


def setup_inputs(seed: int = 0) -> dict:
    key = jax.random.key(seed)
    ks = jax.random.split(key, 32)

    def nrm(k, shape, scale):
        return jax.random.normal(k, shape, jnp.float32) * scale

    D = D_MODEL
    return {
        "x_prompt": nrm(ks[0], (BATCH, SEQ, D), 1.0),
        "x_sample": nrm(ks[1], (DEC_BATCH, DEC_SEQ, D), 1.0),
        "cache_k": nrm(ks[2], (DEC_BATCH, N_EVEN, PAST_LEN, NA_HEADS, HEAD_DIM), 1.0),
        "cache_v": nrm(ks[3], (DEC_BATCH, N_EVEN, PAST_LEN, NA_HEADS, HEAD_DIM), 1.0),
        "c": nrm(ks[4], (DEC_BATCH, D), 1.0),
        "c_ctx": nrm(ks[5], (D,), 1.0),
        "ada_w": nrm(ks[6], (DEPTH, D, 6 * D), 0.5 * D ** -0.5),
        "ada_b": nrm(ks[7], (DEPTH, 6 * D), 0.02),
        "norm1_g": 1.0 + nrm(ks[8], (DEPTH, D), 0.1),
        "norm2_g": 1.0 + nrm(ks[9], (DEPTH, D), 0.1),
        "na_w_in": nrm(ks[10], (N_EVEN, D, 3 * NA_WIDTH + FN_WIDTH), D ** -0.5),
        "na_w_out": nrm(ks[11], (N_EVEN, NA_WIDTH + FN_WIDTH, D), (NA_WIDTH + FN_WIDTH) ** -0.5),
        "q_gain": 1.0 + nrm(ks[12], (N_EVEN, HEAD_DIM), 0.1),
        "k_gain": 1.0 + nrm(ks[13], (N_EVEN, HEAD_DIM), 0.1),
        "rpb": nrm(ks[14], (N_EVEN, NA_HEADS, 2 * WIN_ROWS - 1, 2 * WIN_COLS - 1), 0.1),
        "sg_w_in": nrm(ks[15], (N_ODD, D, 2 * SG_WIDTH), D ** -0.5),
        "sg_w_out": nrm(ks[16], (N_ODD, SG_WIDTH, D), SG_WIDTH ** -0.5),
        "sg_v_gain": 1.0 + nrm(ks[17], (N_ODD, SG_WIDTH), 0.1),
        "sg_w_s": nrm(ks[18], (N_ODD, SG_GROUPS, CHUNK, CHUNK), CHUNK ** -0.5),
        "sg_b_s": 1.0 + nrm(ks[19], (N_ODD, SG_GROUPS, CHUNK), 0.1),
        "router_w": nrm(ks[20], (D, N_EXPERTS), D ** -0.5),
        "router_b": nrm(ks[21], (N_EXPERTS,), 0.01),
        "moe_w_gate": nrm(ks[22], (DEPTH, N_EXPERTS, D, EXPERT_FF), D ** -0.5),
        "moe_w_up": nrm(ks[23], (DEPTH, N_EXPERTS, D, EXPERT_FF), D ** -0.5),
        "moe_w_down": nrm(ks[24], (DEPTH, N_EXPERTS, EXPERT_FF, D), EXPERT_FF ** -0.5),
    }


def reference(x_prompt, x_sample, cache_k, cache_v, c, c_ctx, ada_w, ada_b, norm1_g, norm2_g,
              na_w_in, na_w_out, q_gain, k_gain, rpb, sg_w_in, sg_w_out, sg_v_gain, sg_w_s, sg_b_s,
              router_w, router_b, moe_w_gate, moe_w_up, moe_w_down):
    w = (ada_w, ada_b, norm1_g, norm2_g, na_w_in, na_w_out, q_gain, k_gain, rpb,
         sg_w_in, sg_w_out, sg_v_gain, sg_w_s, sg_b_s, router_w, router_b,
         moe_w_gate, moe_w_up, moe_w_down)
    y_prompt, k_list, v_list = _trunk(x_prompt, c_ctx[None, None, :], None, None, w)
    k_ctx = jnp.stack(k_list, axis=1)
    v_ctx = jnp.stack(v_list, axis=1)
    y_sample, _, _ = _trunk(x_sample, c[:, None, :], cache_k, cache_v, w)
    return (y_prompt, y_sample, k_ctx, v_ctx)
```

```python
import functools

import numpy as np
import jax
import jax.numpy as jnp
from jax import lax
from jax.experimental import pallas as pl
from jax.experimental.pallas import tpu as pltpu

F32 = jnp.float32
BF16 = jnp.bfloat16

D_MODEL = 1024
GRID_W = 64
HEAD_DIM = 64
NA_WIDTH = 512
NA_HEADS = 8
WIN_ROWS = 8
WIN_COLS = 16
FN_WIDTH = 512
FN_GROUP_DIM = 64
SG_WIDTH = 1024
SG_GROUPS = 4
SG_GROUP_DIM = 256
CHUNK = 128
N_EXPERTS = 16
N_EXPERT_GROUPS = 4
EXPERTS_PER_GROUP = 4
EXPERT_FF = 256
EPS = 1e-6
NEG_INF = -1e30

LANES = 128
TM = 512
ROW_PITCH = 72
N_COND_ROWS = 8
CTX_ROW = 4
ADA_TN = 1536
FFT_CB = 256
VMEM_LIMIT = 56 * 1024 * 1024


def _rms_mod(x, g, shift, scale):
    ms = jnp.mean(x * x, axis=-1, keepdims=True)
    y = x * lax.rsqrt(ms + EPS) * g
    return y * (1.0 + scale) + shift


def _mod_chunk(m_ref, i):
    return m_ref[:, i * D_MODEL:(i + 1) * D_MODEL]


def _ada_kernel(c_ref, w_ref, b_ref, o_ref):
    s = jax.nn.silu(c_ref[...]).astype(BF16)
    o_ref[...] = jnp.dot(s, w_ref[...].astype(BF16), preferred_element_type=F32) + b_ref[...]


def _ada_table(cond, ada_w, ada_b):
    depth = ada_w.shape[0]
    n_out = ada_w.shape[2]
    return pl.pallas_call(
        _ada_kernel,
        grid=(depth, n_out // ADA_TN),
        in_specs=[
            pl.BlockSpec((N_COND_ROWS, D_MODEL), lambda l, j: (0, 0)),
            pl.BlockSpec((None, D_MODEL, ADA_TN), lambda l, j: (l, 0, j)),
            pl.BlockSpec((None, 1, ADA_TN), lambda l, j: (l, 0, j)),
        ],
        out_specs=pl.BlockSpec((None, N_COND_ROWS, ADA_TN), lambda l, j: (l, 0, j)),
        out_shape=jax.ShapeDtypeStruct((depth, N_COND_ROWS, n_out), F32),
        name="ada_table",
    )(cond, ada_w, ada_b.reshape(depth, 1, n_out))


def _head_norm(t, bd_ref, gain):
    ms = jnp.dot((t * t).astype(BF16), bd_ref[...], preferred_element_type=F32)
    return t * lax.rsqrt(ms + EPS) * gain


def _inproj_kernel(x_ref, m_ref, g_ref, w_ref, qg_ref, kg_ref, bd_ref, *out_refs, is_ctx):
    h = _rms_mod(x_ref[...], g_ref[...], _mod_chunk(m_ref, 0), _mod_chunk(m_ref, 1)).astype(BF16)
    proj = jnp.dot(h, w_ref[...], preferred_element_type=F32)
    q = _head_norm(proj[:, 0:NA_WIDTH], bd_ref, qg_ref[...]) * (HEAD_DIM ** -0.5)
    k = _head_norm(proj[:, NA_WIDTH:2 * NA_WIDTH], bd_ref, kg_ref[...])
    v = proj[:, 2 * NA_WIDTH:3 * NA_WIDTH]
    u = proj[:, 3 * NA_WIDTH:]
    if is_ctx:
        q_ref, k_ref, v_ref, u_ref, k32_ref, v32_ref = out_refs
        k32_ref[...] = k
        v32_ref[...] = v
        u_ref[...] = u.astype(BF16)
    else:
        q_ref, k_ref, v_ref, u_ref = out_refs
        zeros = jnp.zeros((ROW_PITCH - GRID_W, LANES), F32)
        for r in range(TM // GRID_W):
            for j in range(FN_WIDTH // LANES):
                u_ref[j, r * ROW_PITCH:r * ROW_PITCH + GRID_W, :] = (
                    u[r * GRID_W:(r + 1) * GRID_W, j * LANES:(j + 1) * LANES])
                u_ref[j, r * ROW_PITCH + GRID_W:(r + 1) * ROW_PITCH, :] = zeros
    q_ref[...] = q.astype(BF16)
    k_ref[...] = k.astype(BF16)
    v_ref[...] = v.astype(BF16)


def _mods_spec(layer, row_of_tile):
    return pl.BlockSpec((None, None, 1, 6 * D_MODEL), lambda i: (layer, row_of_tile(i), 0, 0))


def _inproj(x2, mods4, row_of_tile, g, w_in, qg, kg, bd, is_ctx, batch):
    n = x2.shape[0]
    tile = lambda i: (i, 0)
    const = lambda i: (0, 0)
    bf = lambda: jax.ShapeDtypeStruct((n, NA_WIDTH), BF16)
    spec512 = pl.BlockSpec((TM, NA_WIDTH), tile)
    out_shape = [bf(), bf(), bf()]
    out_specs = [spec512, spec512, spec512]
    if is_ctx:
        out_shape += [bf(), jax.ShapeDtypeStruct((n, NA_WIDTH), F32), jax.ShapeDtypeStruct((n, NA_WIDTH), F32)]
        out_specs += [spec512, spec512, spec512]
    else:
        rows = n // batch // GRID_W
        tiles_per_batch = n // batch // TM
        pad_tile = TM // GRID_W * ROW_PITCH
        out_shape += [jax.ShapeDtypeStruct((batch, FN_WIDTH // LANES, rows * ROW_PITCH, LANES), F32)]
        out_specs += [pl.BlockSpec((None, FN_WIDTH // LANES, pad_tile, LANES),
                                   lambda i: (i // tiles_per_batch, 0, i % tiles_per_batch, 0))]
    return pl.pallas_call(
        functools.partial(_inproj_kernel, is_ctx=is_ctx),
        grid=(n // TM,),
        in_specs=[
            pl.BlockSpec((TM, D_MODEL), tile),
            _mods_spec(0, row_of_tile),
            pl.BlockSpec((1, D_MODEL), const),
            pl.BlockSpec((D_MODEL, 4 * NA_WIDTH), const),
            pl.BlockSpec((1, NA_WIDTH), const),
            pl.BlockSpec((1, NA_WIDTH), const),
            pl.BlockSpec((NA_WIDTH, NA_WIDTH), const),
        ],
        out_specs=out_specs,
        out_shape=out_shape,
        name="inproj_ctx" if is_ctx else "inproj_lat",
    )(x2, mods4, g, w_in, qg, kg, bd)


def _softmax_pv(parts):
    m = parts[0][0].max(axis=-1, keepdims=True)
    for s, _ in parts[1:]:
        m = jnp.maximum(m, s.max(axis=-1, keepdims=True))
    l = None
    o = None
    for s, v in parts:
        e = jnp.exp(s - m)
        ls = e.sum(axis=-1, keepdims=True)
        pv = jnp.dot(e.astype(BF16), v, preferred_element_type=F32)
        l = ls if l is None else l + ls
        o = pv if o is None else o + pv
    return o / l


def _nt_dot(a, b):
    return lax.dot_general(a, b, (((1,), (1,)), ((), ())), preferred_element_type=F32)


def _attn_ctx_kernel(q_ref, k_ref, v_ref, o_ref):
    lane = lax.broadcasted_iota(jnp.int32, (q_ref.shape[0], LANES), 1)
    even = lane < HEAD_DIM
    for p in range(NA_HEADS // 2):
        sl = slice(p * LANES, (p + 1) * LANES)
        qp, kp, vp = q_ref[:, sl], k_ref[:, sl], v_ref[:, sl]
        outs = []
        for half in range(2):
            qm = jnp.where(even if half == 0 else ~even, qp, jnp.zeros_like(qp))
            outs.append(_softmax_pv([(_nt_dot(qm, kp), vp)]))
        o_ref[:, sl] = jnp.where(even, outs[0], outs[1]).astype(BF16)


def _attn_ctx(q, k, v, batch):
    n = q.shape[0]
    l = n // batch
    spec = pl.BlockSpec((l, NA_WIDTH), lambda b: (b, 0))
    return pl.pallas_call(
        _attn_ctx_kernel,
        grid=(batch,),
        in_specs=[spec, spec, spec],
        out_specs=spec,
        out_shape=jax.ShapeDtypeStruct((n, NA_WIDTH), BF16),
        name="attn_ctx",
    )(q, k, v)


def _attn_lat_kernel(q_ref, k_ref, v_ref, kc_ref, vc_ref, bias_ref, o_ref, *, rows):
    r = pl.program_id(1)
    row_start = jnp.clip(r - WIN_ROWS // 2, 0, rows - WIN_ROWS)
    start = pl.multiple_of(row_start * GRID_W, GRID_W)
    win = pl.ds(start, WIN_ROWS * GRID_W)
    lane = lax.broadcasted_iota(jnp.int32, (GRID_W, LANES), 1)
    even = lane < HEAD_DIM
    for p in range(NA_HEADS // 2):
        sl = slice(p * LANES, (p + 1) * LANES)
        qp = q_ref[:, sl]
        kw, vw = k_ref[win, sl], v_ref[win, sl]
        kc, vc = kc_ref[:, sl], vc_ref[:, sl]
        outs = []
        for half in range(2):
            qm = jnp.where(even if half == 0 else ~even, qp, jnp.zeros_like(qp))
            s_lat = _nt_dot(qm, kw) + bias_ref[2 * p + half]
            s_ctx = _nt_dot(qm, kc)
            outs.append(_softmax_pv([(s_lat, vw), (s_ctx, vc)]))
        o_ref[:, sl] = jnp.where(even, outs[0], outs[1]).astype(BF16)


def _bias_offset(r, rows):
    return jnp.clip(r - WIN_ROWS // 2, 0, rows - WIN_ROWS) - r + (WIN_ROWS - 1)


def _attn_lat(q, k, v, kc, vc, bias_tbl, batch):
    n = q.shape[0]
    l = n // batch
    rows = l // GRID_W
    q3, k3, v3 = (t.reshape(batch, l, NA_WIDTH) for t in (q, k, v))
    full = pl.BlockSpec((None, l, NA_WIDTH), lambda b, r: (b, 0, 0))
    ctx = pl.BlockSpec((None, kc.shape[1], NA_WIDTH), lambda b, r: (b, 0, 0))
    qspec = pl.BlockSpec((None, GRID_W, NA_WIDTH), lambda b, r: (b, r, 0))
    out = pl.pallas_call(
        functools.partial(_attn_lat_kernel, rows=rows),
        grid=(batch, rows),
        in_specs=[
            qspec, full, full, ctx, ctx,
            pl.BlockSpec((None, NA_HEADS, GRID_W, WIN_ROWS * GRID_W),
                         lambda b, r: (_bias_offset(r, rows), 0, 0, 0)),
        ],
        out_specs=qspec,
        out_shape=jax.ShapeDtypeStruct((batch, l, NA_WIDTH), BF16),
        name="attn_lat",
    )(q3, k3, v3, kc, vc, bias_tbl)
    return out.reshape(n, NA_WIDTH)


def _bias_table(rpb_i):
    col = np.arange(GRID_W)
    dc = np.clip(col[None, :] - col[:, None] + (WIN_COLS - 1), 0, 2 * WIN_COLS - 2)
    col_start = np.clip(col - WIN_COLS // 2, 0, GRID_W - WIN_COLS)
    valid = (col[None, :] >= col_start[:, None]) & (col[None, :] < col_start[:, None] + WIN_COLS)
    dr = np.arange(WIN_ROWS)[:, None] + np.arange(WIN_ROWS)[None, :]
    t = rpb_i[:, dr]
    t = t[:, :, :, dc]
    t = jnp.where(valid[None, None, None], t.astype(F32), NEG_INF)
    t = jnp.transpose(t, (1, 0, 3, 2, 4))
    return t.reshape(WIN_ROWS, NA_HEADS, GRID_W, WIN_ROWS * GRID_W)


def _bf16_table(a):
    return jnp.asarray(a, F32).astype(BF16)


def _dft_cos_sin(n):
    idx = np.arange(n)
    ang = 2.0 * np.pi * ((idx[:, None] * idx[None, :]) % n) / n
    return np.cos(ang), np.sin(ang)


def _channel_dft(width, scale):
    c, s = _dft_cos_sin(FN_GROUP_DIM)
    eye = np.eye(width // FN_GROUP_DIM)
    return np.concatenate([np.kron(eye, c), np.kron(eye, s)], axis=0) * scale


def _fft_ctx_kernel(u_ref, fc_ref, cs_ref, o_ref):
    l = u_ref.shape[0]
    x = jnp.dot(fc_ref[...], u_ref[...], preferred_element_type=F32)
    xri = jnp.concatenate([x[:l], x[l:]], axis=1).astype(BF16)
    o_ref[...] = jnp.dot(xri, cs_ref[...], preferred_element_type=F32).astype(BF16)


def _fft_ctx(u, batch):
    n = u.shape[0]
    l = n // batch
    c, s = _dft_cos_sin(l)
    fc = _bf16_table(np.concatenate([c, -s], axis=0))
    cs = _bf16_table(_channel_dft(FN_WIDTH, (l * FN_GROUP_DIM) ** -0.5))
    spec = pl.BlockSpec((l, FN_WIDTH), lambda b: (b, 0))
    return pl.pallas_call(
        _fft_ctx_kernel,
        grid=(batch,),
        in_specs=[spec, pl.BlockSpec(fc.shape, lambda b: (0, 0)), pl.BlockSpec(cs.shape, lambda b: (0, 0))],
        out_specs=spec,
        out_shape=jax.ShapeDtypeStruct((n, FN_WIDTH), BF16),
        name="fourier_ctx",
    )(u, fc, cs)


def _fft_lat_kernel(u_ref, f1_ref, twr_ref, twi_ref, f2_ref, cs_ref, o_ref, z_ref, x2_ref):
    slabs = FFT_CB // LANES
    n1 = GRID_W
    x2_ref[...] = jnp.zeros(x2_ref.shape, F32)

    def stage1(n2, carry):
        u = jnp.concatenate([u_ref[j, pl.ds(n2, n1, stride=ROW_PITCH), :] for j in range(slabs)], axis=1)
        y = jnp.dot(f1_ref[...], u.astype(BF16), preferred_element_type=F32)
        yr, yi = y[:n1], y[n1:]
        tr = jnp.concatenate([twr_ref[n2]] * slabs, axis=1)
        ti = jnp.concatenate([twi_ref[n2]] * slabs, axis=1)
        zr = yr * tr - yi * ti
        zi = yr * ti + yi * tr
        for j in range(slabs):
            sl = slice(j * LANES, (j + 1) * LANES)
            z_ref[0, j, pl.ds(n2, n1, stride=ROW_PITCH), :] = zr[:, sl]
            z_ref[1, j, pl.ds(n2, n1, stride=ROW_PITCH), :] = zi[:, sl]
        return carry

    lax.fori_loop(0, GRID_W, stage1, 0)

    def stage2(k1, carry):
        base = pl.multiple_of(k1 * ROW_PITCH, 8)
        zr = jnp.concatenate([z_ref[0, j, pl.ds(base, GRID_W), :] for j in range(slabs)], axis=1)
        zi = jnp.concatenate([z_ref[1, j, pl.ds(base, GRID_W), :] for j in range(slabs)], axis=1)
        zz = jnp.concatenate([zr, zi], axis=0).astype(BF16)
        x = jnp.dot(f2_ref[...], zz, preferred_element_type=F32)
        for j in range(slabs):
            sl = slice(j * LANES, (j + 1) * LANES)
            x2_ref[j, pl.ds(k1, GRID_W, stride=ROW_PITCH), :] = x[:GRID_W, sl]
            x2_ref[slabs + j, pl.ds(k1, GRID_W, stride=ROW_PITCH), :] = x[GRID_W:, sl]
        return carry

    lax.fori_loop(0, n1, stage2, 0)

    chunk = 8 * ROW_PITCH
    for c in range(GRID_W * ROW_PITCH // chunk):
        rs = slice(c * chunk, (c + 1) * chunk)
        xri = jnp.concatenate([x2_ref[s, rs, :] for s in range(2 * slabs)], axis=1).astype(BF16)
        o_ref[rs, :] = jnp.dot(xri, cs_ref[...], preferred_element_type=F32)


def _fft_lat(u_pad, batch):
    l = GRID_W * GRID_W
    rows_pad = GRID_W * ROW_PITCH
    slabs = FFT_CB // LANES
    c, s = _dft_cos_sin(GRID_W)
    f1 = _bf16_table(np.concatenate([c, -s], axis=0))
    f2 = _bf16_table(np.block([[c, s], [-s, c]]))
    n2k1 = (np.arange(GRID_W)[:, None] * np.arange(GRID_W)[None, :]) % l
    ang = 2.0 * np.pi * n2k1 / l
    twr = jnp.broadcast_to(jnp.asarray(np.cos(ang), F32)[:, :, None], (GRID_W, GRID_W, LANES))
    twi = jnp.broadcast_to(jnp.asarray(-np.sin(ang), F32)[:, :, None], (GRID_W, GRID_W, LANES))
    cs = _bf16_table(_channel_dft(FFT_CB, (l * FN_GROUP_DIM) ** -0.5))
    const2 = lambda b, cb: (0, 0)
    const3 = lambda b, cb: (0, 0, 0)
    return pl.pallas_call(
        _fft_lat_kernel,
        grid=(batch, FN_WIDTH // FFT_CB),
        in_specs=[
            pl.BlockSpec((None, slabs, rows_pad, LANES), lambda b, cb: (b, cb, 0, 0)),
            pl.BlockSpec(f1.shape, const2),
            pl.BlockSpec(twr.shape, const3),
            pl.BlockSpec(twi.shape, const3),
            pl.BlockSpec(f2.shape, const2),
            pl.BlockSpec(cs.shape, const2),
        ],
        out_specs=pl.BlockSpec((None, rows_pad, FFT_CB), lambda b, cb: (b, 0, cb)),
        out_shape=jax.ShapeDtypeStruct((batch, rows_pad, FN_WIDTH), F32),
        scratch_shapes=[
            pltpu.VMEM((2, slabs, rows_pad, LANES), F32),
            pltpu.VMEM((2 * slabs, rows_pad, LANES), F32),
        ],
        compiler_params=pltpu.CompilerParams(vmem_limit_bytes=VMEM_LIMIT),
        name="fourier_lat",
    )(u_pad, f1, twr, twi, f2, cs)


def _outproj_kernel(x_ref, a_ref, f_ref, m_ref, w_ref, o_ref, *, is_ctx):
    if is_ctx:
        f = f_ref[...]
    else:
        f = jnp.concatenate(
            [f_ref[r * ROW_PITCH:r * ROW_PITCH + GRID_W, :] for r in range(TM // GRID_W)], axis=0).astype(BF16)
    af = jnp.concatenate([a_ref[...], f], axis=1)
    mix = jnp.dot(af, w_ref[...], preferred_element_type=F32)
    o_ref[...] = x_ref[...] + _mod_chunk(m_ref, 2) * mix


def _outproj(x2, a, f, mods4, row_of_tile, w_out, is_ctx, batch):
    n = x2.shape[0]
    tile = lambda i: (i, 0)
    if is_ctx:
        fspec = pl.BlockSpec((TM, FN_WIDTH), tile)
    else:
        tiles_per_batch = n // batch // TM
        fspec = pl.BlockSpec((None, TM // GRID_W * ROW_PITCH, FN_WIDTH),
                             lambda i: (i // tiles_per_batch, i % tiles_per_batch, 0))
    return pl.pallas_call(
        functools.partial(_outproj_kernel, is_ctx=is_ctx),
        grid=(n // TM,),
        in_specs=[
            pl.BlockSpec((TM, D_MODEL), tile),
            pl.BlockSpec((TM, NA_WIDTH), tile),
            fspec,
            _mods_spec(0, row_of_tile),
            pl.BlockSpec((D_MODEL, D_MODEL), lambda i: (0, 0)),
        ],
        out_specs=pl.BlockSpec((TM, D_MODEL), tile),
        out_shape=jax.ShapeDtypeStruct((n, D_MODEL), F32),
        name="outproj_ctx" if is_ctx else "outproj_lat",
    )(x2, a, f, mods4, w_out)


def _sg_kernel(x_ref, m_ref, g_ref, win_ref, vg_ref, ws_ref, bs_ref, wout_ref, o_ref):
    x = x_ref[...]
    h = _rms_mod(x, g_ref[...], _mod_chunk(m_ref, 0), _mod_chunk(m_ref, 1)).astype(BF16)
    z = jax.nn.gelu(jnp.dot(h, win_ref[...], preferred_element_type=F32))
    gated = []
    for g in range(SG_GROUPS):
        gs = slice(g * SG_GROUP_DIM, (g + 1) * SG_GROUP_DIM)
        u = z[:, gs]
        v = z[:, SG_WIDTH + g * SG_GROUP_DIM:SG_WIDTH + (g + 1) * SG_GROUP_DIM]
        ms = jnp.mean(v * v, axis=-1, keepdims=True)
        vn = (v * lax.rsqrt(ms + EPS) * vg_ref[:, gs]).astype(BF16)
        w = ws_ref[g]
        s = jnp.concatenate(
            [jnp.dot(w, vn[c * CHUNK:(c + 1) * CHUNK], preferred_element_type=F32) for c in range(TM // CHUNK)],
            axis=0)
        bias = jnp.concatenate([bs_ref[:, gs]] * (TM // CHUNK), axis=0)
        gated.append((u * (s + bias)).astype(BF16))
    mix = jnp.dot(jnp.concatenate(gated, axis=1), wout_ref[...], preferred_element_type=F32)
    o_ref[...] = x + _mod_chunk(m_ref, 2) * mix


def _spatial_gating(x2, mods4, row_of_tile, g, w_in, v_gain, w_s, b_full, w_out):
    n = x2.shape[0]
    tile = lambda i: (i, 0)
    const = lambda i: (0, 0)
    return pl.pallas_call(
        _sg_kernel,
        grid=(n // TM,),
        in_specs=[
            pl.BlockSpec((TM, D_MODEL), tile),
            _mods_spec(1, row_of_tile),
            pl.BlockSpec((1, D_MODEL), const),
            pl.BlockSpec((D_MODEL, 2 * SG_WIDTH), const),
            pl.BlockSpec((1, SG_WIDTH), const),
            pl.BlockSpec((SG_GROUPS, CHUNK, CHUNK), lambda i: (0, 0, 0)),
            pl.BlockSpec((CHUNK, SG_WIDTH), const),
            pl.BlockSpec((SG_WIDTH, D_MODEL), const),
        ],
        out_specs=pl.BlockSpec((TM, D_MODEL), tile),
        out_shape=jax.ShapeDtypeStruct((n, D_MODEL), F32),
        name="spatial_gating",
    )(x2, mods4, g, w_in, v_gain, w_s, b_full, w_out)


def _first_index_of(mask, idx, size):
    return jnp.min(jnp.where(mask, idx, size), axis=0, keepdims=True)


def _router_kernel(x_ref, m_ref, g_ref, rwt_ref, rb_ref, h_ref, gates_ref):
    h = _rms_mod(x_ref[...], g_ref[...], _mod_chunk(m_ref, 3), _mod_chunk(m_ref, 4))
    h_ref[...] = h.astype(BF16)
    logits = lax.dot_general(rwt_ref[...], h, (((1,), (1,)), ((), ())),
                             precision=lax.Precision.HIGHEST, preferred_element_type=F32)
    scores = jax.nn.sigmoid(logits)
    sel = scores + rb_ref[...]
    eg = EXPERTS_PER_GROUP
    ri = lax.broadcasted_iota(jnp.int32, (eg, TM), 0)
    firsts, seconds, group_scores = [], [], []
    for g in range(N_EXPERT_GROUPS):
        a = sel[g * eg:(g + 1) * eg]
        m1 = a.max(axis=0, keepdims=True)
        i1 = _first_index_of(a == m1, ri, eg)
        rest = jnp.where(ri == i1, -jnp.inf, a)
        m2 = rest.max(axis=0, keepdims=True)
        i2 = _first_index_of(rest == m2, ri, eg)
        firsts.append(i1)
        seconds.append(i2)
        group_scores.append(m1 + m2)
    gs = jnp.concatenate(group_scores, axis=0)
    gi = lax.broadcasted_iota(jnp.int32, gs.shape, 0)
    g_best = _first_index_of(gs == gs.max(axis=0, keepdims=True), gi, N_EXPERT_GROUPS)
    picked = []
    for g in range(N_EXPERT_GROUPS):
        chosen = (g_best == g) & ((ri == firsts[g]) | (ri == seconds[g]))
        picked.append(jnp.where(chosen, scores[g * eg:(g + 1) * eg], 0.0))
    w = jnp.concatenate(picked, axis=0)
    w = w / w.sum(axis=0, keepdims=True)
    wt = jnp.concatenate([w, jnp.zeros((LANES - N_EXPERTS, TM), F32)], axis=0)
    gates_ref[...] = wt.T


def _router(x2, mods4, layer, row_of_tile, g, rwt, rb):
    n = x2.shape[0]
    tile = lambda i: (i, 0)
    const = lambda i: (0, 0)
    return pl.pallas_call(
        _router_kernel,
        grid=(n // TM,),
        in_specs=[
            pl.BlockSpec((TM, D_MODEL), tile),
            _mods_spec(layer, row_of_tile),
            pl.BlockSpec((1, D_MODEL), const),
            pl.BlockSpec((N_EXPERTS, D_MODEL), const),
            pl.BlockSpec((N_EXPERTS, 1), const),
        ],
        out_specs=[pl.BlockSpec((TM, D_MODEL), tile), pl.BlockSpec((TM, LANES), tile)],
        out_shape=[jax.ShapeDtypeStruct((n, D_MODEL), BF16), jax.ShapeDtypeStruct((n, LANES), F32)],
        name="router",
    )(x2, mods4, g, rwt, rb)


def _moe_kernel(x_ref, h_ref, gates_ref, m_ref, wg_ref, wu_ref, wd_ref, o_ref):
    h = h_ref[...]
    gates = gates_ref[...]
    y = jnp.zeros((TM, D_MODEL), F32)
    for e in range(N_EXPERTS):
        a = jnp.dot(h, wg_ref[e], preferred_element_type=F32)
        b = jnp.dot(h, wu_ref[e], preferred_element_type=F32)
        hid = jax.nn.silu(a) * b * gates[:, e:e + 1]
        y = y + jnp.dot(hid.astype(BF16), wd_ref[e], preferred_element_type=F32)
    o_ref[...] = x_ref[...] + _mod_chunk(m_ref, 5) * y


def _moe(x2, h, gates, mods4, layer, row_of_tile, wg, wu, wd):
    n = x2.shape[0]
    tile = lambda i: (i, 0)
    resident = lambda shape: pl.BlockSpec(shape, lambda i: (0, 0, 0), pipeline_mode=pl.Buffered(1))
    return pl.pallas_call(
        _moe_kernel,
        grid=(n // TM,),
        in_specs=[
            pl.BlockSpec((TM, D_MODEL), tile),
            pl.BlockSpec((TM, D_MODEL), tile),
            pl.BlockSpec((TM, LANES), tile),
            _mods_spec(layer, row_of_tile),
            resident(wg.shape), resident(wu.shape), resident(wd.shape),
        ],
        out_specs=pl.BlockSpec((TM, D_MODEL), tile),
        out_shape=jax.ShapeDtypeStruct((n, D_MODEL), F32),
        compiler_params=pltpu.CompilerParams(vmem_limit_bytes=VMEM_LIMIT),
        name="moe",
    )(x2, h, gates, mods4, wg, wu, wd)


def _trunk(x, mods4, row_of_tile, cache, p, is_ctx):
    batch, l, _ = x.shape
    x2 = x.reshape(batch * l, D_MODEL)

    outs = _inproj(x2, mods4, row_of_tile, p["norm1_g"][0], p["na_w_in"], p["q_gain"], p["k_gain"], p["head_bd"],
                   is_ctx, batch)
    if is_ctx:
        q, k, v, u, k32, v32 = outs
        a = _attn_ctx(q, k, v, batch)
        f = _fft_ctx(u, batch)
    else:
        q, k, v, u_pad = outs
        a = _attn_lat(q, k, v, cache[0], cache[1], p["bias_tbl"], batch)
        f = _fft_lat(u_pad, batch)
        k32 = v32 = None
    x2 = _outproj(x2, a, f, mods4, row_of_tile, p["na_w_out"], is_ctx, batch)
    h, gates = _router(x2, mods4, 0, row_of_tile, p["norm2_g"][0], p["router_wt"], p["router_b"])
    x2 = _moe(x2, h, gates, mods4, 0, row_of_tile, p["moe_wg"][0], p["moe_wu"][0], p["moe_wd"][0])

    x2 = _spatial_gating(x2, mods4, row_of_tile, p["norm1_g"][1], p["sg_w_in"], p["sg_v_gain"], p["sg_w_s"],
                         p["sg_b_full"], p["sg_w_out"])
    h, gates = _router(x2, mods4, 1, row_of_tile, p["norm2_g"][1], p["router_wt"], p["router_b"])
    x2 = _moe(x2, h, gates, mods4, 1, row_of_tile, p["moe_wg"][1], p["moe_wu"][1], p["moe_wd"][1])
    return x2.reshape(batch, l, D_MODEL), k32, v32


def kernel(x_prompt, x_sample, cache_k, cache_v, c, c_ctx, ada_w, ada_b, norm1_g, norm2_g, na_w_in, na_w_out,
           q_gain, k_gain, rpb, sg_w_in, sg_w_out, sg_v_gain, sg_w_s, sg_b_s, router_w, router_b,
           moe_w_gate, moe_w_up, moe_w_down):
    batch, seq, _ = x_prompt.shape
    dec_batch, dec_seq, _ = x_sample.shape
    assert dec_batch <= CTX_ROW and dec_seq == GRID_W * GRID_W and (batch * seq) % TM == 0
    assert ada_w.shape[0] == 2 and na_w_in.shape[0] == 1 and sg_w_in.shape[0] == 1

    cond = jnp.zeros((N_COND_ROWS, D_MODEL), F32).at[:dec_batch].set(c).at[CTX_ROW].set(c_ctx)
    mods = _ada_table(cond, ada_w, ada_b)
    mods4 = mods.reshape(mods.shape[0], N_COND_ROWS, 1, mods.shape[2])

    tile_rep = lambda t: jnp.tile(t, NA_HEADS).reshape(1, NA_WIDTH)
    p = {
        "norm1_g": norm1_g.reshape(-1, 1, D_MODEL),
        "norm2_g": norm2_g.reshape(-1, 1, D_MODEL),
        "na_w_in": na_w_in[0].astype(BF16),
        "na_w_out": na_w_out[0].astype(BF16),
        "q_gain": tile_rep(q_gain[0]),
        "k_gain": tile_rep(k_gain[0]),
        "head_bd": jnp.asarray(np.kron(np.eye(NA_HEADS), np.full((HEAD_DIM, HEAD_DIM), 1.0 / HEAD_DIM)), BF16),
        "bias_tbl": _bias_table(rpb[0]),
        "sg_w_in": sg_w_in[0].astype(BF16),
        "sg_w_out": sg_w_out[0].astype(BF16),
        "sg_v_gain": sg_v_gain[0].reshape(1, SG_WIDTH),
        "sg_w_s": sg_w_s[0].astype(BF16),
        "sg_b_full": jnp.broadcast_to(sg_b_s[0].T[:, :, None], (CHUNK, SG_GROUPS, SG_GROUP_DIM)).reshape(
            CHUNK, SG_WIDTH),
        "router_wt": router_w.T,
        "router_b": router_b.reshape(N_EXPERTS, 1),
        "moe_wg": moe_w_gate.astype(BF16),
        "moe_wu": moe_w_up.astype(BF16),
        "moe_wd": moe_w_down.astype(BF16),
    }

    y_prompt, k32, v32 = _trunk(x_prompt, mods4, lambda i: CTX_ROW, None, p, True)
    k_ctx = k32.reshape(batch, 1, seq, NA_HEADS, HEAD_DIM)
    v_ctx = v32.reshape(batch, 1, seq, NA_HEADS, HEAD_DIM)

    tiles_per_batch = dec_seq // TM
    cache = (cache_k[:, 0].reshape(dec_batch, -1, NA_WIDTH).astype(BF16),
             cache_v[:, 0].reshape(dec_batch, -1, NA_WIDTH).astype(BF16))
    y_sample, _, _ = _trunk(x_sample, mods4, lambda i: i // tiles_per_batch, cache, p, False)
    return (y_prompt, y_sample, k_ctx, v_ctx)
```

```python
import functools

import numpy as np
import jax
import jax.numpy as jnp
from jax import lax
from jax.experimental import pallas as pl
from jax.experimental.pallas import tpu as pltpu

F32 = jnp.float32
BF16 = jnp.bfloat16

D_MODEL = 1024
GRID_W = 64
HEAD_DIM = 64
NA_WIDTH = 512
NA_HEADS = 8
WIN_ROWS = 8
WIN_COLS = 16
FN_WIDTH = 512
FN_GROUP_DIM = 64
SG_WIDTH = 1024
SG_GROUPS = 4
SG_GROUP_DIM = 256
CHUNK = 128
N_EXPERTS = 16
N_EXPERT_GROUPS = 4
EXPERTS_PER_GROUP = 4
EXPERT_FF = 256
EPS = 1e-6
NEG_INF = -1e30

LANES = 128
TM = 512
ROW_GROUP = 4
GROUP_WIN_ROWS = 12
ROW_PITCH = 72
N_COND_ROWS = 8
CTX_ROW = 4
ADA_TN = 1536
FFT_CB = 256
VMEM_LIMIT = 56 * 1024 * 1024


def _rms_mod(x, g, shift, scale):
    ms = jnp.mean(x * x, axis=-1, keepdims=True)
    y = x * lax.rsqrt(ms + EPS) * g
    return y * (1.0 + scale) + shift


def _mod_chunk(m_ref, i):
    return m_ref[:, i * D_MODEL:(i + 1) * D_MODEL]


def _ada_kernel(c_ref, w_ref, b_ref, o_ref):
    s = jax.nn.silu(c_ref[...]).astype(BF16)
    o_ref[...] = jnp.dot(s, w_ref[...].astype(BF16), preferred_element_type=F32) + b_ref[...]


def _ada_table(cond, ada_w, ada_b):
    depth = ada_w.shape[0]
    n_out = ada_w.shape[2]
    return pl.pallas_call(
        _ada_kernel,
        grid=(depth, n_out // ADA_TN),
        in_specs=[
            pl.BlockSpec((N_COND_ROWS, D_MODEL), lambda l, j: (0, 0)),
            pl.BlockSpec((None, D_MODEL, ADA_TN), lambda l, j: (l, 0, j)),
            pl.BlockSpec((None, 1, ADA_TN), lambda l, j: (l, 0, j)),
        ],
        out_specs=pl.BlockSpec((None, N_COND_ROWS, ADA_TN), lambda l, j: (l, 0, j)),
        out_shape=jax.ShapeDtypeStruct((depth, N_COND_ROWS, n_out), F32),
        name="ada_table",
    )(cond, ada_w, ada_b.reshape(depth, 1, n_out))


def _head_norm(t, bd_ref, gain):
    ms = jnp.dot((t * t).astype(BF16), bd_ref[...], preferred_element_type=F32)
    return t * lax.rsqrt(ms + EPS) * gain


def _inproj_kernel(x_ref, m_ref, g_ref, w_ref, qg_ref, kg_ref, bd_ref, *out_refs, is_ctx):
    h = _rms_mod(x_ref[...], g_ref[...], _mod_chunk(m_ref, 0), _mod_chunk(m_ref, 1)).astype(BF16)
    proj = jnp.dot(h, w_ref[...], preferred_element_type=F32)
    q = _head_norm(proj[:, 0:NA_WIDTH], bd_ref, qg_ref[...]) * (HEAD_DIM ** -0.5)
    k = _head_norm(proj[:, NA_WIDTH:2 * NA_WIDTH], bd_ref, kg_ref[...])
    v = proj[:, 2 * NA_WIDTH:3 * NA_WIDTH]
    u = proj[:, 3 * NA_WIDTH:]
    if is_ctx:
        q_ref, k_ref, v_ref, u_ref, k32_ref, v32_ref = out_refs
        k32_ref[...] = k
        v32_ref[...] = v
        u_ref[...] = u.astype(BF16)
    else:
        q_ref, k_ref, v_ref, u_ref = out_refs
        zeros = jnp.zeros((ROW_PITCH - GRID_W, LANES), F32)
        for r in range(TM // GRID_W):
            for j in range(FN_WIDTH // LANES):
                u_ref[j, r * ROW_PITCH:r * ROW_PITCH + GRID_W, :] = (
                    u[r * GRID_W:(r + 1) * GRID_W, j * LANES:(j + 1) * LANES])
                u_ref[j, r * ROW_PITCH + GRID_W:(r + 1) * ROW_PITCH, :] = zeros
    q_ref[...] = q.astype(BF16)
    k_ref[...] = k.astype(BF16)
    v_ref[...] = v.astype(BF16)


def _mods_spec(layer, row_of_tile):
    return pl.BlockSpec((None, None, 1, 6 * D_MODEL), lambda i: (layer, row_of_tile(i), 0, 0))


def _inproj(x2, mods4, row_of_tile, g, w_in, qg, kg, bd, is_ctx, batch):
    n = x2.shape[0]
    tile = lambda i: (i, 0)
    const = lambda i: (0, 0)
    bf = lambda: jax.ShapeDtypeStruct((n, NA_WIDTH), BF16)
    spec512 = pl.BlockSpec((TM, NA_WIDTH), tile)
    out_shape = [bf(), bf(), bf()]
    out_specs = [spec512, spec512, spec512]
    if is_ctx:
        out_shape += [bf(), jax.ShapeDtypeStruct((n, NA_WIDTH), F32), jax.ShapeDtypeStruct((n, NA_WIDTH), F32)]
        out_specs += [spec512, spec512, spec512]
    else:
        rows = n // batch // GRID_W
        tiles_per_batch = n // batch // TM
        pad_tile = TM // GRID_W * ROW_PITCH
        out_shape += [jax.ShapeDtypeStruct((batch, FN_WIDTH // LANES, rows * ROW_PITCH, LANES), F32)]
        out_specs += [pl.BlockSpec((None, FN_WIDTH // LANES, pad_tile, LANES),
                                   lambda i: (i // tiles_per_batch, 0, i % tiles_per_batch, 0))]
    return pl.pallas_call(
        functools.partial(_inproj_kernel, is_ctx=is_ctx),
        grid=(n // TM,),
        in_specs=[
            pl.BlockSpec((TM, D_MODEL), tile),
            _mods_spec(0, row_of_tile),
            pl.BlockSpec((1, D_MODEL), const),
            pl.BlockSpec((D_MODEL, 4 * NA_WIDTH), const),
            pl.BlockSpec((1, NA_WIDTH), const),
            pl.BlockSpec((1, NA_WIDTH), const),
            pl.BlockSpec((NA_WIDTH, NA_WIDTH), const),
        ],
        out_specs=out_specs,
        out_shape=out_shape,
        name="inproj_ctx" if is_ctx else "inproj_lat",
    )(x2, mods4, g, w_in, qg, kg, bd)


def _softmax_pv(parts):
    m = parts[0][0].max(axis=-1, keepdims=True)
    for s, _ in parts[1:]:
        m = jnp.maximum(m, s.max(axis=-1, keepdims=True))
    l = None
    o = None
    for s, v in parts:
        e = jnp.exp(s - m)
        ls = e.sum(axis=-1, keepdims=True)
        pv = jnp.dot(e.astype(BF16), v, preferred_element_type=F32)
        l = ls if l is None else l + ls
        o = pv if o is None else o + pv
    return o / l


def _nt_dot(a, b):
    return lax.dot_general(a, b, (((1,), (1,)), ((), ())), preferred_element_type=F32)


def _attn_ctx_kernel(q_ref, k_ref, v_ref, o_ref):
    lane = lax.broadcasted_iota(jnp.int32, (q_ref.shape[0], LANES), 1)
    even = lane < HEAD_DIM
    for p in range(NA_HEADS // 2):
        sl = slice(p * LANES, (p + 1) * LANES)
        qp, kp, vp = q_ref[:, sl], k_ref[:, sl], v_ref[:, sl]
        outs = []
        for half in range(2):
            qm = jnp.where(even if half == 0 else ~even, qp, jnp.zeros_like(qp))
            outs.append(_softmax_pv([(_nt_dot(qm, kp), vp)]))
        o_ref[:, sl] = jnp.where(even, outs[0], outs[1]).astype(BF16)


def _attn_ctx(q, k, v, batch):
    n = q.shape[0]
    l = n // batch
    spec = pl.BlockSpec((l, NA_WIDTH), lambda b: (b, 0))
    return pl.pallas_call(
        _attn_ctx_kernel,
        grid=(batch,),
        in_specs=[spec, spec, spec],
        out_specs=spec,
        out_shape=jax.ShapeDtypeStruct((n, NA_WIDTH), BF16),
        name="attn_ctx",
    )(q, k, v)


def _attn_lat_kernel(q_ref, k_ref, v_ref, kc_ref, vc_ref, bias_ref, o_ref, *, rows):
    g = pl.program_id(1)
    base = jnp.clip(ROW_GROUP * g - WIN_ROWS // 2, 0, rows - GROUP_WIN_ROWS)
    start = pl.multiple_of(base * GRID_W, GRID_W)
    win = pl.ds(start, GROUP_WIN_ROWS * GRID_W)
    lane = lax.broadcasted_iota(jnp.int32, (ROW_GROUP * GRID_W, LANES), 1)
    even = lane < HEAD_DIM
    for p in range(NA_HEADS // 2):
        sl = slice(p * LANES, (p + 1) * LANES)
        qp = q_ref[:, sl]
        kw, vw = k_ref[win, sl], v_ref[win, sl]
        kc, vc = kc_ref[:, sl], vc_ref[:, sl]
        outs = []
        for half in range(2):
            qm = jnp.where(even if half == 0 else ~even, qp, jnp.zeros_like(qp))
            s_lat = _nt_dot(qm, kw) + bias_ref[2 * p + half]
            s_ctx = _nt_dot(qm, kc)
            outs.append(_softmax_pv([(s_lat, vw), (s_ctx, vc)]))
        o_ref[:, sl] = jnp.where(even, outs[0], outs[1]).astype(BF16)


def _attn_lat(q, k, v, kc, vc, bias_tbl, batch):
    n = q.shape[0]
    l = n // batch
    rows = l // GRID_W
    groups = rows // ROW_GROUP
    q3, k3, v3 = (t.reshape(batch, l, NA_WIDTH) for t in (q, k, v))
    full = pl.BlockSpec((None, l, NA_WIDTH), lambda b, g: (b, 0, 0))
    ctx = pl.BlockSpec((None, kc.shape[1], NA_WIDTH), lambda b, g: (b, 0, 0))
    qspec = pl.BlockSpec((None, ROW_GROUP * GRID_W, NA_WIDTH), lambda b, g: (b, g, 0))
    bias_class = lambda b, g: (jnp.where(g == 0, 0, jnp.where(g == groups - 1, 2, 1)), 0, 0, 0)
    out = pl.pallas_call(
        functools.partial(_attn_lat_kernel, rows=rows),
        grid=(batch, groups),
        in_specs=[
            qspec, full, full, ctx, ctx,
            pl.BlockSpec((None, NA_HEADS, ROW_GROUP * GRID_W, GROUP_WIN_ROWS * GRID_W), bias_class),
        ],
        out_specs=qspec,
        out_shape=jax.ShapeDtypeStruct((batch, l, NA_WIDTH), BF16),
        compiler_params=pltpu.CompilerParams(vmem_limit_bytes=VMEM_LIMIT),
        name="attn_lat",
    )(q3, k3, v3, kc, vc, bias_tbl)
    return out.reshape(n, NA_WIDTH)


def _bias_table(rpb_i, rows):
    groups = rows // ROW_GROUP
    assert rows % ROW_GROUP == 0 and rows >= GROUP_WIN_ROWS and groups >= 3
    col = np.arange(GRID_W)
    dc = np.clip(col[None, :] - col[:, None] + (WIN_COLS - 1), 0, 2 * WIN_COLS - 2)
    col_start = np.clip(col - WIN_COLS // 2, 0, GRID_W - WIN_COLS)
    col_valid = (col[None, :] >= col_start[:, None]) & (col[None, :] < col_start[:, None] + WIN_COLS)
    col_onehot = (dc[:, :, None] == np.arange(2 * WIN_COLS - 1)).astype(np.float32)
    row_onehot = np.zeros((3, ROW_GROUP, GROUP_WIN_ROWS, 2 * WIN_ROWS - 1), np.float32)
    row_valid = np.zeros((3, ROW_GROUP, GROUP_WIN_ROWS), bool)
    for cls, g in enumerate((0, 1, groups - 1)):
        base = np.clip(ROW_GROUP * g - WIN_ROWS // 2, 0, rows - GROUP_WIN_ROWS)
        for j in range(ROW_GROUP):
            r = ROW_GROUP * g + j
            row_start = np.clip(r - WIN_ROWS // 2, 0, rows - WIN_ROWS)
            for u in range(GROUP_WIN_ROWS):
                if row_start <= base + u < row_start + WIN_ROWS:
                    row_onehot[cls, j, u, base + u - r + WIN_ROWS - 1] = 1.0
                    row_valid[cls, j, u] = True
    t = jnp.einsum("hrc,xjur,qkc->xhjquk", rpb_i.astype(F32), row_onehot, col_onehot,
                   precision=lax.Precision.HIGHEST)
    valid = row_valid[:, None, :, None, :, None] & col_valid[None, None, None, :, None, :]
    t = jnp.where(valid, t, NEG_INF)
    return t.reshape(3, NA_HEADS, ROW_GROUP * GRID_W, GROUP_WIN_ROWS * GRID_W)


def _bf16_table(a):
    return jnp.asarray(a, F32).astype(BF16)


def _dft_cos_sin(n):
    idx = np.arange(n)
    ang = 2.0 * np.pi * ((idx[:, None] * idx[None, :]) % n) / n
    return np.cos(ang), np.sin(ang)


def _channel_dft(width, scale):
    c, s = _dft_cos_sin(FN_GROUP_DIM)
    eye = np.eye(width // FN_GROUP_DIM)
    return np.concatenate([np.kron(eye, c), np.kron(eye, s)], axis=0) * scale


def _fft_ctx_kernel(u_ref, fc_ref, cs_ref, o_ref):
    l = u_ref.shape[0]
    x = jnp.dot(fc_ref[...], u_ref[...], preferred_element_type=F32)
    xri = jnp.concatenate([x[:l], x[l:]], axis=1).astype(BF16)
    o_ref[...] = jnp.dot(xri, cs_ref[...], preferred_element_type=F32).astype(BF16)


def _fft_ctx(u, batch):
    n = u.shape[0]
    l = n // batch
    c, s = _dft_cos_sin(l)
    fc = _bf16_table(np.concatenate([c, -s], axis=0))
    cs = _bf16_table(_channel_dft(FN_WIDTH, (l * FN_GROUP_DIM) ** -0.5))
    spec = pl.BlockSpec((l, FN_WIDTH), lambda b: (b, 0))
    return pl.pallas_call(
        _fft_ctx_kernel,
        grid=(batch,),
        in_specs=[spec, pl.BlockSpec(fc.shape, lambda b: (0, 0)), pl.BlockSpec(cs.shape, lambda b: (0, 0))],
        out_specs=spec,
        out_shape=jax.ShapeDtypeStruct((n, FN_WIDTH), BF16),
        name="fourier_ctx",
    )(u, fc, cs)


def _fft_lat_kernel(u_ref, f1_ref, twr_ref, twi_ref, f2_ref, cs_ref, o_ref, z_ref, x2_ref):
    slabs = FFT_CB // LANES
    n1 = GRID_W
    x2_ref[...] = jnp.zeros(x2_ref.shape, F32)

    def stage1(n2, carry):
        u = jnp.concatenate([u_ref[j, pl.ds(n2, n1, stride=ROW_PITCH), :] for j in range(slabs)], axis=1)
        y = jnp.dot(f1_ref[...], u.astype(BF16), preferred_element_type=F32)
        yr, yi = y[:n1], y[n1:]
        tr = jnp.concatenate([twr_ref[n2]] * slabs, axis=1)
        ti = jnp.concatenate([twi_ref[n2]] * slabs, axis=1)
        zr = yr * tr - yi * ti
        zi = yr * ti + yi * tr
        for j in range(slabs):
            sl = slice(j * LANES, (j + 1) * LANES)
            z_ref[0, j, pl.ds(n2, n1, stride=ROW_PITCH), :] = zr[:, sl]
            z_ref[1, j, pl.ds(n2, n1, stride=ROW_PITCH), :] = zi[:, sl]
        return carry

    lax.fori_loop(0, GRID_W, stage1, 0)

    def stage2(k1, carry):
        base = pl.multiple_of(k1 * ROW_PITCH, 8)
        zr = jnp.concatenate([z_ref[0, j, pl.ds(base, GRID_W), :] for j in range(slabs)], axis=1)
        zi = jnp.concatenate([z_ref[1, j, pl.ds(base, GRID_W), :] for j in range(slabs)], axis=1)
        zz = jnp.concatenate([zr, zi], axis=0).astype(BF16)
        x = jnp.dot(f2_ref[...], zz, preferred_element_type=F32)
        for j in range(slabs):
            sl = slice(j * LANES, (j + 1) * LANES)
            x2_ref[j, pl.ds(k1, GRID_W, stride=ROW_PITCH), :] = x[:GRID_W, sl]
            x2_ref[slabs + j, pl.ds(k1, GRID_W, stride=ROW_PITCH), :] = x[GRID_W:, sl]
        return carry

    lax.fori_loop(0, n1, stage2, 0)

    chunk = 8 * ROW_PITCH
    for c in range(GRID_W * ROW_PITCH // chunk):
        rs = slice(c * chunk, (c + 1) * chunk)
        xri = jnp.concatenate([x2_ref[s, rs, :] for s in range(2 * slabs)], axis=1).astype(BF16)
        o_ref[rs, :] = jnp.dot(xri, cs_ref[...], preferred_element_type=F32)


def _fft_lat(u_pad, batch):
    l = GRID_W * GRID_W
    rows_pad = GRID_W * ROW_PITCH
    slabs = FFT_CB // LANES
    c, s = _dft_cos_sin(GRID_W)
    f1 = _bf16_table(np.concatenate([c, -s], axis=0))
    f2 = _bf16_table(np.block([[c, s], [-s, c]]))
    n2k1 = (np.arange(GRID_W)[:, None] * np.arange(GRID_W)[None, :]) % l
    ang = 2.0 * np.pi * n2k1 / l
    twr = jnp.broadcast_to(jnp.asarray(np.cos(ang), F32)[:, :, None], (GRID_W, GRID_W, LANES))
    twi = jnp.broadcast_to(jnp.asarray(-np.sin(ang), F32)[:, :, None], (GRID_W, GRID_W, LANES))
    cs = _bf16_table(_channel_dft(FFT_CB, (l * FN_GROUP_DIM) ** -0.5))
    const2 = lambda b, cb: (0, 0)
    const3 = lambda b, cb: (0, 0, 0)
    return pl.pallas_call(
        _fft_lat_kernel,
        grid=(batch, FN_WIDTH // FFT_CB),
        in_specs=[
            pl.BlockSpec((None, slabs, rows_pad, LANES), lambda b, cb: (b, cb, 0, 0)),
            pl.BlockSpec(f1.shape, const2),
            pl.BlockSpec(twr.shape, const3),
            pl.BlockSpec(twi.shape, const3),
            pl.BlockSpec(f2.shape, const2),
            pl.BlockSpec(cs.shape, const2),
        ],
        out_specs=pl.BlockSpec((None, rows_pad, FFT_CB), lambda b, cb: (b, 0, cb)),
        out_shape=jax.ShapeDtypeStruct((batch, rows_pad, FN_WIDTH), F32),
        scratch_shapes=[
            pltpu.VMEM((2, slabs, rows_pad, LANES), F32),
            pltpu.VMEM((2 * slabs, rows_pad, LANES), F32),
        ],
        compiler_params=pltpu.CompilerParams(vmem_limit_bytes=VMEM_LIMIT),
        name="fourier_lat",
    )(u_pad, f1, twr, twi, f2, cs)


def _outproj_kernel(x_ref, a_ref, f_ref, m_ref, w_ref, o_ref, *, is_ctx):
    if is_ctx:
        f = f_ref[...]
    else:
        f = jnp.concatenate(
            [f_ref[r * ROW_PITCH:r * ROW_PITCH + GRID_W, :] for r in range(TM // GRID_W)], axis=0).astype(BF16)
    af = jnp.concatenate([a_ref[...], f], axis=1)
    mix = jnp.dot(af, w_ref[...], preferred_element_type=F32)
    o_ref[...] = x_ref[...] + _mod_chunk(m_ref, 2) * mix


def _outproj(x2, a, f, mods4, row_of_tile, w_out, is_ctx, batch):
    n = x2.shape[0]
    tile = lambda i: (i, 0)
    if is_ctx:
        fspec = pl.BlockSpec((TM, FN_WIDTH), tile)
    else:
        tiles_per_batch = n // batch // TM
        fspec = pl.BlockSpec((None, TM // GRID_W * ROW_PITCH, FN_WIDTH),
                             lambda i: (i // tiles_per_batch, i % tiles_per_batch, 0))
    return pl.pallas_call(
        functools.partial(_outproj_kernel, is_ctx=is_ctx),
        grid=(n // TM,),
        in_specs=[
            pl.BlockSpec((TM, D_MODEL), tile),
            pl.BlockSpec((TM, NA_WIDTH), tile),
            fspec,
            _mods_spec(0, row_of_tile),
            pl.BlockSpec((D_MODEL, D_MODEL), lambda i: (0, 0)),
        ],
        out_specs=pl.BlockSpec((TM, D_MODEL), tile),
        out_shape=jax.ShapeDtypeStruct((n, D_MODEL), F32),
        name="outproj_ctx" if is_ctx else "outproj_lat",
    )(x2, a, f, mods4, w_out)


def _sg_kernel(x_ref, m_ref, g_ref, win_ref, vg_ref, ws_ref, bs_ref, wout_ref, o_ref):
    x = x_ref[...]
    h = _rms_mod(x, g_ref[...], _mod_chunk(m_ref, 0), _mod_chunk(m_ref, 1)).astype(BF16)
    z = jax.nn.gelu(jnp.dot(h, win_ref[...], preferred_element_type=F32))
    gated = []
    for g in range(SG_GROUPS):
        gs = slice(g * SG_GROUP_DIM, (g + 1) * SG_GROUP_DIM)
        u = z[:, gs]
        v = z[:, SG_WIDTH + g * SG_GROUP_DIM:SG_WIDTH + (g + 1) * SG_GROUP_DIM]
        ms = jnp.mean(v * v, axis=-1, keepdims=True)
        vn = (v * lax.rsqrt(ms + EPS) * vg_ref[:, gs]).astype(BF16)
        w = ws_ref[g]
        s = jnp.concatenate(
            [jnp.dot(w, vn[c * CHUNK:(c + 1) * CHUNK], preferred_element_type=F32) for c in range(TM // CHUNK)],
            axis=0)
        bias = jnp.concatenate([bs_ref[:, gs]] * (TM // CHUNK), axis=0)
        gated.append((u * (s + bias)).astype(BF16))
    mix = jnp.dot(jnp.concatenate(gated, axis=1), wout_ref[...], preferred_element_type=F32)
    o_ref[...] = x + _mod_chunk(m_ref, 2) * mix


def _spatial_gating(x2, mods4, row_of_tile, g, w_in, v_gain, w_s, b_full, w_out):
    n = x2.shape[0]
    tile = lambda i: (i, 0)
    const = lambda i: (0, 0)
    return pl.pallas_call(
        _sg_kernel,
        grid=(n // TM,),
        in_specs=[
            pl.BlockSpec((TM, D_MODEL), tile),
            _mods_spec(1, row_of_tile),
            pl.BlockSpec((1, D_MODEL), const),
            pl.BlockSpec((D_MODEL, 2 * SG_WIDTH), const),
            pl.BlockSpec((1, SG_WIDTH), const),
            pl.BlockSpec((SG_GROUPS, CHUNK, CHUNK), lambda i: (0, 0, 0)),
            pl.BlockSpec((CHUNK, SG_WIDTH), const),
            pl.BlockSpec((SG_WIDTH, D_MODEL), const),
        ],
        out_specs=pl.BlockSpec((TM, D_MODEL), tile),
        out_shape=jax.ShapeDtypeStruct((n, D_MODEL), F32),
        name="spatial_gating",
    )(x2, mods4, g, w_in, v_gain, w_s, b_full, w_out)


def _first_index_of(mask, idx, size):
    return jnp.min(jnp.where(mask, idx, size), axis=0, keepdims=True)


def _router_kernel(x_ref, m_ref, g_ref, rwt_ref, rb_ref, h_ref, gates_ref):
    h = _rms_mod(x_ref[...], g_ref[...], _mod_chunk(m_ref, 3), _mod_chunk(m_ref, 4))
    h_ref[...] = h.astype(BF16)
    logits = lax.dot_general(rwt_ref[...], h, (((1,), (1,)), ((), ())),
                             precision=lax.Precision.HIGHEST, preferred_element_type=F32)
    scores = jax.nn.sigmoid(logits)
    sel = scores + rb_ref[...]
    eg = EXPERTS_PER_GROUP
    ri = lax.broadcasted_iota(jnp.int32, (eg, TM), 0)
    firsts, seconds, group_scores = [], [], []
    for g in range(N_EXPERT_GROUPS):
        a = sel[g * eg:(g + 1) * eg]
        m1 = a.max(axis=0, keepdims=True)
        i1 = _first_index_of(a == m1, ri, eg)
        rest = jnp.where(ri == i1, -jnp.inf, a)
        m2 = rest.max(axis=0, keepdims=True)
        i2 = _first_index_of(rest == m2, ri, eg)
        firsts.append(i1)
        seconds.append(i2)
        group_scores.append(m1 + m2)
    gs = jnp.concatenate(group_scores, axis=0)
    gi = lax.broadcasted_iota(jnp.int32, gs.shape, 0)
    g_best = _first_index_of(gs == gs.max(axis=0, keepdims=True), gi, N_EXPERT_GROUPS)
    picked = []
    for g in range(N_EXPERT_GROUPS):
        chosen = (g_best == g) & ((ri == firsts[g]) | (ri == seconds[g]))
        picked.append(jnp.where(chosen, scores[g * eg:(g + 1) * eg], 0.0))
    w = jnp.concatenate(picked, axis=0)
    w = w / w.sum(axis=0, keepdims=True)
    wt = jnp.concatenate([w, jnp.zeros((LANES - N_EXPERTS, TM), F32)], axis=0)
    gates_ref[...] = wt.T


def _router(x2, mods4, layer, row_of_tile, g, rwt, rb):
    n = x2.shape[0]
    tile = lambda i: (i, 0)
    const = lambda i: (0, 0)
    return pl.pallas_call(
        _router_kernel,
        grid=(n // TM,),
        in_specs=[
            pl.BlockSpec((TM, D_MODEL), tile),
            _mods_spec(layer, row_of_tile),
            pl.BlockSpec((1, D_MODEL), const),
            pl.BlockSpec((N_EXPERTS, D_MODEL), const),
            pl.BlockSpec((N_EXPERTS, 1), const),
        ],
        out_specs=[pl.BlockSpec((TM, D_MODEL), tile), pl.BlockSpec((TM, LANES), tile)],
        out_shape=[jax.ShapeDtypeStruct((n, D_MODEL), BF16), jax.ShapeDtypeStruct((n, LANES), F32)],
        name="router",
    )(x2, mods4, g, rwt, rb)


def _moe_kernel(x_ref, h_ref, gates_ref, m_ref, wg_ref, wu_ref, wd_ref, o_ref):
    h = h_ref[...]
    gates = gates_ref[...]
    y = jnp.zeros((TM, D_MODEL), F32)
    for e in range(N_EXPERTS):
        a = jnp.dot(h, wg_ref[e], preferred_element_type=F32)
        b = jnp.dot(h, wu_ref[e], preferred_element_type=F32)
        hid = jax.nn.silu(a) * b * gates[:, e:e + 1]
        y = y + jnp.dot(hid.astype(BF16), wd_ref[e], preferred_element_type=F32)
    o_ref[...] = x_ref[...] + _mod_chunk(m_ref, 5) * y


def _moe(x2, h, gates, mods4, layer, row_of_tile, wg, wu, wd):
    n = x2.shape[0]
    tile = lambda i: (i, 0)
    resident = lambda shape: pl.BlockSpec(shape, lambda i: (0, 0, 0), pipeline_mode=pl.Buffered(1))
    return pl.pallas_call(
        _moe_kernel,
        grid=(n // TM,),
        in_specs=[
            pl.BlockSpec((TM, D_MODEL), tile),
            pl.BlockSpec((TM, D_MODEL), tile),
            pl.BlockSpec((TM, LANES), tile),
            _mods_spec(layer, row_of_tile),
            resident(wg.shape), resident(wu.shape), resident(wd.shape),
        ],
        out_specs=pl.BlockSpec((TM, D_MODEL), tile),
        out_shape=jax.ShapeDtypeStruct((n, D_MODEL), F32),
        compiler_params=pltpu.CompilerParams(vmem_limit_bytes=VMEM_LIMIT),
        name="moe",
    )(x2, h, gates, mods4, wg, wu, wd)


def _trunk(x, mods4, row_of_tile, cache, p, is_ctx):
    batch, l, _ = x.shape
    x2 = x.reshape(batch * l, D_MODEL)

    outs = _inproj(x2, mods4, row_of_tile, p["norm1_g"][0], p["na_w_in"], p["q_gain"], p["k_gain"], p["head_bd"],
                   is_ctx, batch)
    if is_ctx:
        q, k, v, u, k32, v32 = outs
        a = _attn_ctx(q, k, v, batch)
        f = _fft_ctx(u, batch)
    else:
        q, k, v, u_pad = outs
        a = _attn_lat(q, k, v, cache[0], cache[1], p["bias_tbl"], batch)
        f = _fft_lat(u_pad, batch)
        k32 = v32 = None
    x2 = _outproj(x2, a, f, mods4, row_of_tile, p["na_w_out"], is_ctx, batch)
    h, gates = _router(x2, mods4, 0, row_of_tile, p["norm2_g"][0], p["router_wt"], p["router_b"])
    x2 = _moe(x2, h, gates, mods4, 0, row_of_tile, p["moe_wg"][0], p["moe_wu"][0], p["moe_wd"][0])

    x2 = _spatial_gating(x2, mods4, row_of_tile, p["norm1_g"][1], p["sg_w_in"], p["sg_v_gain"], p["sg_w_s"],
                         p["sg_b_full"], p["sg_w_out"])
    h, gates = _router(x2, mods4, 1, row_of_tile, p["norm2_g"][1], p["router_wt"], p["router_b"])
    x2 = _moe(x2, h, gates, mods4, 1, row_of_tile, p["moe_wg"][1], p["moe_wu"][1], p["moe_wd"][1])
    return x2.reshape(batch, l, D_MODEL), k32, v32


def kernel(x_prompt, x_sample, cache_k, cache_v, c, c_ctx, ada_w, ada_b, norm1_g, norm2_g, na_w_in, na_w_out,
           q_gain, k_gain, rpb, sg_w_in, sg_w_out, sg_v_gain, sg_w_s, sg_b_s, router_w, router_b,
           moe_w_gate, moe_w_up, moe_w_down):
    batch, seq, _ = x_prompt.shape
    dec_batch, dec_seq, _ = x_sample.shape
    assert dec_batch <= CTX_ROW and dec_seq == GRID_W * GRID_W and (batch * seq) % TM == 0
    assert ada_w.shape[0] == 2 and na_w_in.shape[0] == 1 and sg_w_in.shape[0] == 1

    cond = jnp.zeros((N_COND_ROWS, D_MODEL), F32).at[:dec_batch].set(c).at[CTX_ROW].set(c_ctx)
    mods = _ada_table(cond, ada_w, ada_b)
    mods4 = mods.reshape(mods.shape[0], N_COND_ROWS, 1, mods.shape[2])

    tile_rep = lambda t: jnp.tile(t, NA_HEADS).reshape(1, NA_WIDTH)
    p = {
        "norm1_g": norm1_g.reshape(-1, 1, D_MODEL),
        "norm2_g": norm2_g.reshape(-1, 1, D_MODEL),
        "na_w_in": na_w_in[0].astype(BF16),
        "na_w_out": na_w_out[0].astype(BF16),
        "q_gain": tile_rep(q_gain[0]),
        "k_gain": tile_rep(k_gain[0]),
        "head_bd": jnp.asarray(np.kron(np.eye(NA_HEADS), np.full((HEAD_DIM, HEAD_DIM), 1.0 / HEAD_DIM)), BF16),
        "bias_tbl": _bias_table(rpb[0], dec_seq // GRID_W),
        "sg_w_in": sg_w_in[0].astype(BF16),
        "sg_w_out": sg_w_out[0].astype(BF16),
        "sg_v_gain": sg_v_gain[0].reshape(1, SG_WIDTH),
        "sg_w_s": sg_w_s[0].astype(BF16),
        "sg_b_full": jnp.broadcast_to(sg_b_s[0].T[:, :, None], (CHUNK, SG_GROUPS, SG_GROUP_DIM)).reshape(
            CHUNK, SG_WIDTH),
        "router_wt": router_w.T,
        "router_b": router_b.reshape(N_EXPERTS, 1),
        "moe_wg": moe_w_gate.astype(BF16),
        "moe_wu": moe_w_up.astype(BF16),
        "moe_wd": moe_w_down.astype(BF16),
    }

    y_prompt, k32, v32 = _trunk(x_prompt, mods4, lambda i: CTX_ROW, None, p, True)
    k_ctx = k32.reshape(batch, 1, seq, NA_HEADS, HEAD_DIM)
    v_ctx = v32.reshape(batch, 1, seq, NA_HEADS, HEAD_DIM)

    tiles_per_batch = dec_seq // TM
    cache = (cache_k[:, 0].reshape(dec_batch, -1, NA_WIDTH).astype(BF16),
             cache_v[:, 0].reshape(dec_batch, -1, NA_WIDTH).astype(BF16))
    y_sample, _, _ = _trunk(x_sample, mods4, lambda i: i // tiles_per_batch, cache, p, False)
    return (y_prompt, y_sample, k_ctx, v_ctx)
```

```python
import functools

import numpy as np
import jax
import jax.numpy as jnp
from jax import lax
from jax.experimental import pallas as pl
from jax.experimental.pallas import tpu as pltpu

F32 = jnp.float32
BF16 = jnp.bfloat16

D_MODEL = 1024
GRID_W = 64
HEAD_DIM = 64
NA_WIDTH = 512
NA_HEADS = 8
WIN_ROWS = 8
WIN_COLS = 16
FN_WIDTH = 512
FN_GROUP_DIM = 64
SG_WIDTH = 1024
SG_GROUPS = 4
SG_GROUP_DIM = 256
CHUNK = 128
N_EXPERTS = 16
N_EXPERT_GROUPS = 4
EXPERTS_PER_GROUP = 4
EXPERT_FF = 256
EPS = 1e-6
NEG_INF = -1e30

LANES = 128
TM = 512
ROW_GROUP = 4
GROUP_WIN_ROWS = 12
ROW_PITCH = 72
N_COND_ROWS = 8
CTX_ROW = 4
ADA_TN = 1536
FFT_UNROLL = 4
FFT_CB = 256
VMEM_LIMIT = 56 * 1024 * 1024


def _rms_mod(x, g, shift, scale):
    ms = jnp.mean(x * x, axis=-1, keepdims=True)
    y = x * lax.rsqrt(ms + EPS) * g
    return y * (1.0 + scale) + shift


def _mod_chunk(m_ref, i):
    return m_ref[:, i * D_MODEL:(i + 1) * D_MODEL]


def _ada_kernel(c_ref, w_ref, b_ref, o_ref):
    s = jax.nn.silu(c_ref[...]).astype(BF16)
    o_ref[...] = jnp.dot(s, w_ref[...].astype(BF16), preferred_element_type=F32) + b_ref[...]


def _ada_table(cond, ada_w, ada_b):
    depth = ada_w.shape[0]
    n_out = ada_w.shape[2]
    return pl.pallas_call(
        _ada_kernel,
        grid=(depth, n_out // ADA_TN),
        in_specs=[
            pl.BlockSpec((N_COND_ROWS, D_MODEL), lambda l, j: (0, 0)),
            pl.BlockSpec((None, D_MODEL, ADA_TN), lambda l, j: (l, 0, j)),
            pl.BlockSpec((None, 1, ADA_TN), lambda l, j: (l, 0, j)),
        ],
        out_specs=pl.BlockSpec((None, N_COND_ROWS, ADA_TN), lambda l, j: (l, 0, j)),
        out_shape=jax.ShapeDtypeStruct((depth, N_COND_ROWS, n_out), F32),
        name="ada_table",
    )(cond, ada_w, ada_b.reshape(depth, 1, n_out))


def _head_norm(t, bd_ref, gain):
    ms = jnp.dot((t * t).astype(BF16), bd_ref[...], preferred_element_type=F32)
    return t * lax.rsqrt(ms + EPS) * gain


def _inproj_kernel(x_ref, m_ref, g_ref, w_ref, qg_ref, kg_ref, bd_ref, *out_refs, is_ctx):
    h = _rms_mod(x_ref[...], g_ref[...], _mod_chunk(m_ref, 0), _mod_chunk(m_ref, 1)).astype(BF16)
    proj = jnp.dot(h, w_ref[...], preferred_element_type=F32)
    q = _head_norm(proj[:, 0:NA_WIDTH], bd_ref, qg_ref[...]) * (HEAD_DIM ** -0.5)
    k = _head_norm(proj[:, NA_WIDTH:2 * NA_WIDTH], bd_ref, kg_ref[...])
    v = proj[:, 2 * NA_WIDTH:3 * NA_WIDTH]
    u = proj[:, 3 * NA_WIDTH:]
    if is_ctx:
        q_ref, k_ref, v_ref, u_ref, k32_ref, v32_ref = out_refs
        k32_ref[...] = k
        v32_ref[...] = v
        u_ref[...] = u.astype(BF16)
    else:
        q_ref, k_ref, v_ref, u_ref = out_refs
        zeros = jnp.zeros((ROW_PITCH - GRID_W, LANES), F32)
        for r in range(TM // GRID_W):
            for j in range(FN_WIDTH // LANES):
                u_ref[j, r * ROW_PITCH:r * ROW_PITCH + GRID_W, :] = (
                    u[r * GRID_W:(r + 1) * GRID_W, j * LANES:(j + 1) * LANES])
                u_ref[j, r * ROW_PITCH + GRID_W:(r + 1) * ROW_PITCH, :] = zeros
    q_ref[...] = q.astype(BF16)
    k_ref[...] = k.astype(BF16)
    v_ref[...] = v.astype(BF16)


def _mods_spec(layer, row_of_tile):
    return pl.BlockSpec((None, None, 1, 6 * D_MODEL), lambda i: (layer, row_of_tile(i), 0, 0))


def _inproj(x2, mods4, row_of_tile, g, w_in, qg, kg, bd, is_ctx, batch):
    n = x2.shape[0]
    tile = lambda i: (i, 0)
    const = lambda i: (0, 0)
    bf = lambda: jax.ShapeDtypeStruct((n, NA_WIDTH), BF16)
    spec512 = pl.BlockSpec((TM, NA_WIDTH), tile)
    out_shape = [bf(), bf(), bf()]
    out_specs = [spec512, spec512, spec512]
    if is_ctx:
        out_shape += [bf(), jax.ShapeDtypeStruct((n, NA_WIDTH), F32), jax.ShapeDtypeStruct((n, NA_WIDTH), F32)]
        out_specs += [spec512, spec512, spec512]
    else:
        rows = n // batch // GRID_W
        tiles_per_batch = n // batch // TM
        pad_tile = TM // GRID_W * ROW_PITCH
        out_shape += [jax.ShapeDtypeStruct((batch, FN_WIDTH // LANES, rows * ROW_PITCH, LANES), F32)]
        out_specs += [pl.BlockSpec((None, FN_WIDTH // LANES, pad_tile, LANES),
                                   lambda i: (i // tiles_per_batch, 0, i % tiles_per_batch, 0))]
    return pl.pallas_call(
        functools.partial(_inproj_kernel, is_ctx=is_ctx),
        grid=(n // TM,),
        in_specs=[
            pl.BlockSpec((TM, D_MODEL), tile),
            _mods_spec(0, row_of_tile),
            pl.BlockSpec((1, D_MODEL), const),
            pl.BlockSpec((D_MODEL, 4 * NA_WIDTH), const),
            pl.BlockSpec((1, NA_WIDTH), const),
            pl.BlockSpec((1, NA_WIDTH), const),
            pl.BlockSpec((NA_WIDTH, NA_WIDTH), const),
        ],
        out_specs=out_specs,
        out_shape=out_shape,
        name="inproj_ctx" if is_ctx else "inproj_lat",
    )(x2, mods4, g, w_in, qg, kg, bd)


def _softmax_pv(parts):
    m = parts[0][0].max(axis=-1, keepdims=True)
    for s, _ in parts[1:]:
        m = jnp.maximum(m, s.max(axis=-1, keepdims=True))
    l = None
    o = None
    for s, v in parts:
        e = jnp.exp(s - m)
        ls = e.sum(axis=-1, keepdims=True)
        pv = jnp.dot(e.astype(BF16), v, preferred_element_type=F32)
        l = ls if l is None else l + ls
        o = pv if o is None else o + pv
    return o / l


def _nt_dot(a, b):
    return lax.dot_general(a, b, (((1,), (1,)), ((), ())), preferred_element_type=F32)


def _attn_ctx_kernel(q_ref, k_ref, v_ref, o_ref):
    lane = lax.broadcasted_iota(jnp.int32, (q_ref.shape[0], LANES), 1)
    even = lane < HEAD_DIM
    for p in range(NA_HEADS // 2):
        sl = slice(p * LANES, (p + 1) * LANES)
        qp, kp, vp = q_ref[:, sl], k_ref[:, sl], v_ref[:, sl]
        outs = []
        for half in range(2):
            qm = jnp.where(even if half == 0 else ~even, qp, jnp.zeros_like(qp))
            outs.append(_softmax_pv([(_nt_dot(qm, kp), vp)]))
        o_ref[:, sl] = jnp.where(even, outs[0], outs[1]).astype(BF16)


def _attn_ctx(q, k, v, batch):
    n = q.shape[0]
    l = n // batch
    spec = pl.BlockSpec((l, NA_WIDTH), lambda b: (b, 0))
    return pl.pallas_call(
        _attn_ctx_kernel,
        grid=(batch,),
        in_specs=[spec, spec, spec],
        out_specs=spec,
        out_shape=jax.ShapeDtypeStruct((n, NA_WIDTH), BF16),
        name="attn_ctx",
    )(q, k, v)


def _attn_lat_kernel(q_ref, k_ref, v_ref, kc_ref, vc_ref, bias_ref, o_ref, *, rows):
    g = pl.program_id(1)
    base = jnp.clip(ROW_GROUP * g - WIN_ROWS // 2, 0, rows - GROUP_WIN_ROWS)
    start = pl.multiple_of(base * GRID_W, GRID_W)
    win = pl.ds(start, GROUP_WIN_ROWS * GRID_W)
    lane = lax.broadcasted_iota(jnp.int32, (ROW_GROUP * GRID_W, LANES), 1)
    even = lane < HEAD_DIM
    for p in range(NA_HEADS // 2):
        sl = slice(p * LANES, (p + 1) * LANES)
        qp = q_ref[:, sl]
        kw, vw = k_ref[win, sl], v_ref[win, sl]
        kc, vc = kc_ref[:, sl], vc_ref[:, sl]
        outs = []
        for half in range(2):
            qm = jnp.where(even if half == 0 else ~even, qp, jnp.zeros_like(qp))
            s_lat = _nt_dot(qm, kw) + bias_ref[2 * p + half]
            s_ctx = _nt_dot(qm, kc)
            outs.append(_softmax_pv([(s_lat, vw), (s_ctx, vc)]))
        o_ref[:, sl] = jnp.where(even, outs[0], outs[1]).astype(BF16)


def _attn_lat(q, k, v, kc, vc, bias_tbl, batch):
    n = q.shape[0]
    l = n // batch
    rows = l // GRID_W
    groups = rows // ROW_GROUP
    q3, k3, v3 = (t.reshape(batch, l, NA_WIDTH) for t in (q, k, v))
    full = pl.BlockSpec((None, l, NA_WIDTH), lambda b, g: (b, 0, 0))
    ctx = pl.BlockSpec((None, kc.shape[1], NA_WIDTH), lambda b, g: (b, 0, 0))
    qspec = pl.BlockSpec((None, ROW_GROUP * GRID_W, NA_WIDTH), lambda b, g: (b, g, 0))
    bias_class = lambda b, g: (jnp.where(g == 0, 0, jnp.where(g == groups - 1, 2, 1)), 0, 0, 0)
    out = pl.pallas_call(
        functools.partial(_attn_lat_kernel, rows=rows),
        grid=(batch, groups),
        in_specs=[
            qspec, full, full, ctx, ctx,
            pl.BlockSpec((None, NA_HEADS, ROW_GROUP * GRID_W, GROUP_WIN_ROWS * GRID_W), bias_class),
        ],
        out_specs=qspec,
        out_shape=jax.ShapeDtypeStruct((batch, l, NA_WIDTH), BF16),
        compiler_params=pltpu.CompilerParams(vmem_limit_bytes=VMEM_LIMIT),
        name="attn_lat",
    )(q3, k3, v3, kc, vc, bias_tbl)
    return out.reshape(n, NA_WIDTH)


def _bias_table(rpb_i, rows):
    groups = rows // ROW_GROUP
    assert rows % ROW_GROUP == 0 and rows >= GROUP_WIN_ROWS and groups >= 3
    col = np.arange(GRID_W)
    dc = np.clip(col[None, :] - col[:, None] + (WIN_COLS - 1), 0, 2 * WIN_COLS - 2)
    col_start = np.clip(col - WIN_COLS // 2, 0, GRID_W - WIN_COLS)
    col_valid = (col[None, :] >= col_start[:, None]) & (col[None, :] < col_start[:, None] + WIN_COLS)
    col_onehot = (dc[:, :, None] == np.arange(2 * WIN_COLS - 1)).astype(np.float32)
    row_onehot = np.zeros((3, ROW_GROUP, GROUP_WIN_ROWS, 2 * WIN_ROWS - 1), np.float32)
    row_valid = np.zeros((3, ROW_GROUP, GROUP_WIN_ROWS), bool)
    for cls, g in enumerate((0, 1, groups - 1)):
        base = np.clip(ROW_GROUP * g - WIN_ROWS // 2, 0, rows - GROUP_WIN_ROWS)
        for j in range(ROW_GROUP):
            r = ROW_GROUP * g + j
            row_start = np.clip(r - WIN_ROWS // 2, 0, rows - WIN_ROWS)
            for u in range(GROUP_WIN_ROWS):
                if row_start <= base + u < row_start + WIN_ROWS:
                    row_onehot[cls, j, u, base + u - r + WIN_ROWS - 1] = 1.0
                    row_valid[cls, j, u] = True
    t = jnp.einsum("hrc,xjur,qkc->xhjquk", rpb_i.astype(F32), row_onehot, col_onehot,
                   precision=lax.Precision.HIGHEST)
    valid = row_valid[:, None, :, None, :, None] & col_valid[None, None, None, :, None, :]
    t = jnp.where(valid, t, NEG_INF)
    return t.reshape(3, NA_HEADS, ROW_GROUP * GRID_W, GROUP_WIN_ROWS * GRID_W)


def _bf16_table(a):
    return jnp.asarray(a, F32).astype(BF16)


def _dft_cos_sin(n):
    idx = np.arange(n)
    ang = 2.0 * np.pi * ((idx[:, None] * idx[None, :]) % n) / n
    return np.cos(ang), np.sin(ang)


def _channel_dft(width, scale):
    c, s = _dft_cos_sin(FN_GROUP_DIM)
    eye = np.eye(width // FN_GROUP_DIM)
    return np.concatenate([np.kron(eye, c), np.kron(eye, s)], axis=0) * scale


def _fft_ctx_kernel(u_ref, fc_ref, cs_ref, o_ref):
    l = u_ref.shape[0]
    x = jnp.dot(fc_ref[...], u_ref[...], preferred_element_type=F32)
    xri = jnp.concatenate([x[:l], x[l:]], axis=1).astype(BF16)
    o_ref[...] = jnp.dot(xri, cs_ref[...], preferred_element_type=F32).astype(BF16)


def _fft_ctx(u, batch):
    n = u.shape[0]
    l = n // batch
    c, s = _dft_cos_sin(l)
    fc = _bf16_table(np.concatenate([c, -s], axis=0))
    cs = _bf16_table(_channel_dft(FN_WIDTH, (l * FN_GROUP_DIM) ** -0.5))
    spec = pl.BlockSpec((l, FN_WIDTH), lambda b: (b, 0))
    return pl.pallas_call(
        _fft_ctx_kernel,
        grid=(batch,),
        in_specs=[spec, pl.BlockSpec(fc.shape, lambda b: (0, 0)), pl.BlockSpec(cs.shape, lambda b: (0, 0))],
        out_specs=spec,
        out_shape=jax.ShapeDtypeStruct((n, FN_WIDTH), BF16),
        name="fourier_ctx",
    )(u, fc, cs)


def _fft_lat_kernel(u_ref, f1_ref, twr_ref, twi_ref, f2_ref, cs_ref, o_ref, z_ref, x2_ref):
    slabs = FFT_CB // LANES
    n1 = GRID_W
    x2_ref[...] = jnp.zeros(x2_ref.shape, F32)

    def stage1(n2, carry):
        u = jnp.concatenate([u_ref[j, pl.ds(n2, n1, stride=ROW_PITCH), :] for j in range(slabs)], axis=1)
        y = jnp.dot(f1_ref[...], u.astype(BF16), preferred_element_type=F32)
        yr, yi = y[:n1], y[n1:]
        tr = jnp.concatenate([twr_ref[n2]] * slabs, axis=1)
        ti = jnp.concatenate([twi_ref[n2]] * slabs, axis=1)
        zr = yr * tr - yi * ti
        zi = yr * ti + yi * tr
        for j in range(slabs):
            sl = slice(j * LANES, (j + 1) * LANES)
            z_ref[0, j, pl.ds(n2, n1, stride=ROW_PITCH), :] = zr[:, sl]
            z_ref[1, j, pl.ds(n2, n1, stride=ROW_PITCH), :] = zi[:, sl]
        return carry

    lax.fori_loop(0, GRID_W, stage1, 0, unroll=FFT_UNROLL)

    def stage2(k1, carry):
        base = pl.multiple_of(k1 * ROW_PITCH, 8)
        zr = jnp.concatenate([z_ref[0, j, pl.ds(base, GRID_W), :] for j in range(slabs)], axis=1)
        zi = jnp.concatenate([z_ref[1, j, pl.ds(base, GRID_W), :] for j in range(slabs)], axis=1)
        zz = jnp.concatenate([zr, zi], axis=0).astype(BF16)
        x = jnp.dot(f2_ref[...], zz, preferred_element_type=F32)
        for j in range(slabs):
            sl = slice(j * LANES, (j + 1) * LANES)
            x2_ref[j, pl.ds(k1, GRID_W, stride=ROW_PITCH), :] = x[:GRID_W, sl]
            x2_ref[slabs + j, pl.ds(k1, GRID_W, stride=ROW_PITCH), :] = x[GRID_W:, sl]
        return carry

    lax.fori_loop(0, n1, stage2, 0, unroll=FFT_UNROLL)

    chunk = 8 * ROW_PITCH
    for c in range(GRID_W * ROW_PITCH // chunk):
        rs = slice(c * chunk, (c + 1) * chunk)
        xri = jnp.concatenate([x2_ref[s, rs, :] for s in range(2 * slabs)], axis=1).astype(BF16)
        o_ref[rs, :] = jnp.dot(xri, cs_ref[...], preferred_element_type=F32)


def _fft_lat(u_pad, batch):
    l = GRID_W * GRID_W
    rows_pad = GRID_W * ROW_PITCH
    slabs = FFT_CB // LANES
    c, s = _dft_cos_sin(GRID_W)
    f1 = _bf16_table(np.concatenate([c, -s], axis=0))
    f2 = _bf16_table(np.block([[c, s], [-s, c]]))
    n2k1 = (np.arange(GRID_W)[:, None] * np.arange(GRID_W)[None, :]) % l
    ang = 2.0 * np.pi * n2k1 / l
    twr = jnp.broadcast_to(jnp.asarray(np.cos(ang), F32)[:, :, None], (GRID_W, GRID_W, LANES))
    twi = jnp.broadcast_to(jnp.asarray(-np.sin(ang), F32)[:, :, None], (GRID_W, GRID_W, LANES))
    cs = _bf16_table(_channel_dft(FFT_CB, (l * FN_GROUP_DIM) ** -0.5))
    const2 = lambda b, cb: (0, 0)
    const3 = lambda b, cb: (0, 0, 0)
    return pl.pallas_call(
        _fft_lat_kernel,
        grid=(batch, FN_WIDTH // FFT_CB),
        in_specs=[
            pl.BlockSpec((None, slabs, rows_pad, LANES), lambda b, cb: (b, cb, 0, 0)),
            pl.BlockSpec(f1.shape, const2),
            pl.BlockSpec(twr.shape, const3),
            pl.BlockSpec(twi.shape, const3),
            pl.BlockSpec(f2.shape, const2),
            pl.BlockSpec(cs.shape, const2),
        ],
        out_specs=pl.BlockSpec((None, rows_pad, FFT_CB), lambda b, cb: (b, 0, cb)),
        out_shape=jax.ShapeDtypeStruct((batch, rows_pad, FN_WIDTH), F32),
        scratch_shapes=[
            pltpu.VMEM((2, slabs, rows_pad, LANES), F32),
            pltpu.VMEM((2 * slabs, rows_pad, LANES), F32),
        ],
        compiler_params=pltpu.CompilerParams(vmem_limit_bytes=VMEM_LIMIT),
        name="fourier_lat",
    )(u_pad, f1, twr, twi, f2, cs)


def _outproj_kernel(x_ref, a_ref, f_ref, m_ref, w_ref, o_ref, *, is_ctx):
    if is_ctx:
        f = f_ref[...]
    else:
        f = jnp.concatenate(
            [f_ref[r * ROW_PITCH:r * ROW_PITCH + GRID_W, :] for r in range(TM // GRID_W)], axis=0).astype(BF16)
    af = jnp.concatenate([a_ref[...], f], axis=1)
    mix = jnp.dot(af, w_ref[...], preferred_element_type=F32)
    o_ref[...] = x_ref[...] + _mod_chunk(m_ref, 2) * mix


def _outproj(x2, a, f, mods4, row_of_tile, w_out, is_ctx, batch):
    n = x2.shape[0]
    tile = lambda i: (i, 0)
    if is_ctx:
        fspec = pl.BlockSpec((TM, FN_WIDTH), tile)
    else:
        tiles_per_batch = n // batch // TM
        fspec = pl.BlockSpec((None, TM // GRID_W * ROW_PITCH, FN_WIDTH),
                             lambda i: (i // tiles_per_batch, i % tiles_per_batch, 0))
    return pl.pallas_call(
        functools.partial(_outproj_kernel, is_ctx=is_ctx),
        grid=(n // TM,),
        in_specs=[
            pl.BlockSpec((TM, D_MODEL), tile),
            pl.BlockSpec((TM, NA_WIDTH), tile),
            fspec,
            _mods_spec(0, row_of_tile),
            pl.BlockSpec((D_MODEL, D_MODEL), lambda i: (0, 0)),
        ],
        out_specs=pl.BlockSpec((TM, D_MODEL), tile),
        out_shape=jax.ShapeDtypeStruct((n, D_MODEL), F32),
        name="outproj_ctx" if is_ctx else "outproj_lat",
    )(x2, a, f, mods4, w_out)


def _sg_kernel(x_ref, m_ref, g_ref, win_ref, vg_ref, ws_ref, bs_ref, wout_ref, o_ref):
    x = x_ref[...]
    h = _rms_mod(x, g_ref[...], _mod_chunk(m_ref, 0), _mod_chunk(m_ref, 1)).astype(BF16)
    z = jax.nn.gelu(jnp.dot(h, win_ref[...], preferred_element_type=F32))
    gated = []
    for g in range(SG_GROUPS):
        gs = slice(g * SG_GROUP_DIM, (g + 1) * SG_GROUP_DIM)
        u = z[:, gs]
        v = z[:, SG_WIDTH + g * SG_GROUP_DIM:SG_WIDTH + (g + 1) * SG_GROUP_DIM]
        ms = jnp.mean(v * v, axis=-1, keepdims=True)
        vn = (v * lax.rsqrt(ms + EPS) * vg_ref[:, gs]).astype(BF16)
        w = ws_ref[g]
        s = jnp.concatenate(
            [jnp.dot(w, vn[c * CHUNK:(c + 1) * CHUNK], preferred_element_type=F32) for c in range(TM // CHUNK)],
            axis=0)
        bias = jnp.concatenate([bs_ref[:, gs]] * (TM // CHUNK), axis=0)
        gated.append((u * (s + bias)).astype(BF16))
    mix = jnp.dot(jnp.concatenate(gated, axis=1), wout_ref[...], preferred_element_type=F32)
    o_ref[...] = x + _mod_chunk(m_ref, 2) * mix


def _spatial_gating(x2, mods4, row_of_tile, g, w_in, v_gain, w_s, b_full, w_out):
    n = x2.shape[0]
    tile = lambda i: (i, 0)
    const = lambda i: (0, 0)
    return pl.pallas_call(
        _sg_kernel,
        grid=(n // TM,),
        in_specs=[
            pl.BlockSpec((TM, D_MODEL), tile),
            _mods_spec(1, row_of_tile),
            pl.BlockSpec((1, D_MODEL), const),
            pl.BlockSpec((D_MODEL, 2 * SG_WIDTH), const),
            pl.BlockSpec((1, SG_WIDTH), const),
            pl.BlockSpec((SG_GROUPS, CHUNK, CHUNK), lambda i: (0, 0, 0)),
            pl.BlockSpec((CHUNK, SG_WIDTH), const),
            pl.BlockSpec((SG_WIDTH, D_MODEL), const),
        ],
        out_specs=pl.BlockSpec((TM, D_MODEL), tile),
        out_shape=jax.ShapeDtypeStruct((n, D_MODEL), F32),
        name="spatial_gating",
    )(x2, mods4, g, w_in, v_gain, w_s, b_full, w_out)


def _first_index_of(mask, idx, size):
    return jnp.min(jnp.where(mask, idx, size), axis=0, keepdims=True)


def _router_kernel(x_ref, m_ref, g_ref, rw_ref, rb_ref, h_ref, gates_ref):
    h = _rms_mod(x_ref[...], g_ref[...], _mod_chunk(m_ref, 3), _mod_chunk(m_ref, 4))
    h_hi = h.astype(BF16)
    h_ref[...] = h_hi
    h_lo = (h - h_hi.astype(F32)).astype(BF16)
    rw = rw_ref[...]
    rw_hi = rw.astype(BF16)
    rw_lo = (rw - rw_hi.astype(F32)).astype(BF16)
    logits = (jnp.dot(h_hi, rw_hi, preferred_element_type=F32)
              + jnp.dot(h_lo, rw_hi, preferred_element_type=F32)
              + jnp.dot(h_hi, rw_lo, preferred_element_type=F32))
    logits = logits.T[:N_EXPERTS]
    scores = jax.nn.sigmoid(logits)
    sel = scores + rb_ref[...]
    eg = EXPERTS_PER_GROUP
    ri = lax.broadcasted_iota(jnp.int32, (eg, TM), 0)
    firsts, seconds, group_scores = [], [], []
    for g in range(N_EXPERT_GROUPS):
        a = sel[g * eg:(g + 1) * eg]
        m1 = a.max(axis=0, keepdims=True)
        i1 = _first_index_of(a == m1, ri, eg)
        rest = jnp.where(ri == i1, -jnp.inf, a)
        m2 = rest.max(axis=0, keepdims=True)
        i2 = _first_index_of(rest == m2, ri, eg)
        firsts.append(i1)
        seconds.append(i2)
        group_scores.append(m1 + m2)
    gs = jnp.concatenate(group_scores, axis=0)
    gi = lax.broadcasted_iota(jnp.int32, gs.shape, 0)
    g_best = _first_index_of(gs == gs.max(axis=0, keepdims=True), gi, N_EXPERT_GROUPS)
    picked = []
    for g in range(N_EXPERT_GROUPS):
        chosen = (g_best == g) & ((ri == firsts[g]) | (ri == seconds[g]))
        picked.append(jnp.where(chosen, scores[g * eg:(g + 1) * eg], 0.0))
    w = jnp.concatenate(picked, axis=0)
    w = w / w.sum(axis=0, keepdims=True)
    wt = jnp.concatenate([w, jnp.zeros((LANES - N_EXPERTS, TM), F32)], axis=0)
    gates_ref[...] = wt.T


def _router(x2, mods4, layer, row_of_tile, g, rw, rb):
    n = x2.shape[0]
    tile = lambda i: (i, 0)
    const = lambda i: (0, 0)
    return pl.pallas_call(
        _router_kernel,
        grid=(n // TM,),
        in_specs=[
            pl.BlockSpec((TM, D_MODEL), tile),
            _mods_spec(layer, row_of_tile),
            pl.BlockSpec((1, D_MODEL), const),
            pl.BlockSpec((D_MODEL, LANES), const),
            pl.BlockSpec((N_EXPERTS, 1), const),
        ],
        out_specs=[pl.BlockSpec((TM, D_MODEL), tile), pl.BlockSpec((TM, LANES), tile)],
        out_shape=[jax.ShapeDtypeStruct((n, D_MODEL), BF16), jax.ShapeDtypeStruct((n, LANES), F32)],
        name="router",
    )(x2, mods4, g, rw, rb)


def _moe_kernel(x_ref, h_ref, gates_ref, m_ref, wg_ref, wu_ref, wd_ref, o_ref):
    h = h_ref[...]
    gates = gates_ref[...]
    y = jnp.zeros((TM, D_MODEL), F32)
    for e in range(N_EXPERTS):
        a = jnp.dot(h, wg_ref[e], preferred_element_type=F32)
        b = jnp.dot(h, wu_ref[e], preferred_element_type=F32)
        hid = jax.nn.silu(a) * b * gates[:, e:e + 1]
        y = y + jnp.dot(hid.astype(BF16), wd_ref[e], preferred_element_type=F32)
    o_ref[...] = x_ref[...] + _mod_chunk(m_ref, 5) * y


def _moe(x2, h, gates, mods4, layer, row_of_tile, wg, wu, wd):
    n = x2.shape[0]
    tile = lambda i: (i, 0)
    resident = lambda shape: pl.BlockSpec(shape, lambda i: (0, 0, 0), pipeline_mode=pl.Buffered(1))
    return pl.pallas_call(
        _moe_kernel,
        grid=(n // TM,),
        in_specs=[
            pl.BlockSpec((TM, D_MODEL), tile),
            pl.BlockSpec((TM, D_MODEL), tile),
            pl.BlockSpec((TM, LANES), tile),
            _mods_spec(layer, row_of_tile),
            resident(wg.shape), resident(wu.shape), resident(wd.shape),
        ],
        out_specs=pl.BlockSpec((TM, D_MODEL), tile),
        out_shape=jax.ShapeDtypeStruct((n, D_MODEL), F32),
        compiler_params=pltpu.CompilerParams(vmem_limit_bytes=VMEM_LIMIT),
        name="moe",
    )(x2, h, gates, mods4, wg, wu, wd)


def _trunk(x, mods4, row_of_tile, cache, p, is_ctx):
    batch, l, _ = x.shape
    x2 = x.reshape(batch * l, D_MODEL)

    outs = _inproj(x2, mods4, row_of_tile, p["norm1_g"][0], p["na_w_in"], p["q_gain"], p["k_gain"], p["head_bd"],
                   is_ctx, batch)
    if is_ctx:
        q, k, v, u, k32, v32 = outs
        a = _attn_ctx(q, k, v, batch)
        f = _fft_ctx(u, batch)
    else:
        q, k, v, u_pad = outs
        a = _attn_lat(q, k, v, cache[0], cache[1], p["bias_tbl"], batch)
        f = _fft_lat(u_pad, batch)
        k32 = v32 = None
    x2 = _outproj(x2, a, f, mods4, row_of_tile, p["na_w_out"], is_ctx, batch)
    h, gates = _router(x2, mods4, 0, row_of_tile, p["norm2_g"][0], p["router_w"], p["router_b"])
    x2 = _moe(x2, h, gates, mods4, 0, row_of_tile, p["moe_wg"][0], p["moe_wu"][0], p["moe_wd"][0])

    x2 = _spatial_gating(x2, mods4, row_of_tile, p["norm1_g"][1], p["sg_w_in"], p["sg_v_gain"], p["sg_w_s"],
                         p["sg_b_full"], p["sg_w_out"])
    h, gates = _router(x2, mods4, 1, row_of_tile, p["norm2_g"][1], p["router_w"], p["router_b"])
    x2 = _moe(x2, h, gates, mods4, 1, row_of_tile, p["moe_wg"][1], p["moe_wu"][1], p["moe_wd"][1])
    return x2.reshape(batch, l, D_MODEL), k32, v32


def kernel(x_prompt, x_sample, cache_k, cache_v, c, c_ctx, ada_w, ada_b, norm1_g, norm2_g, na_w_in, na_w_out,
           q_gain, k_gain, rpb, sg_w_in, sg_w_out, sg_v_gain, sg_w_s, sg_b_s, router_w, router_b,
           moe_w_gate, moe_w_up, moe_w_down):
    batch, seq, _ = x_prompt.shape
    dec_batch, dec_seq, _ = x_sample.shape
    assert dec_batch <= CTX_ROW and dec_seq == GRID_W * GRID_W and (batch * seq) % TM == 0
    assert ada_w.shape[0] == 2 and na_w_in.shape[0] == 1 and sg_w_in.shape[0] == 1

    cond = jnp.zeros((N_COND_ROWS, D_MODEL), F32).at[:dec_batch].set(c).at[CTX_ROW].set(c_ctx)
    mods = _ada_table(cond, ada_w, ada_b)
    mods4 = mods.reshape(mods.shape[0], N_COND_ROWS, 1, mods.shape[2])

    tile_rep = lambda t: jnp.tile(t, NA_HEADS).reshape(1, NA_WIDTH)
    p = {
        "norm1_g": norm1_g.reshape(-1, 1, D_MODEL),
        "norm2_g": norm2_g.reshape(-1, 1, D_MODEL),
        "na_w_in": na_w_in[0].astype(BF16),
        "na_w_out": na_w_out[0].astype(BF16),
        "q_gain": tile_rep(q_gain[0]),
        "k_gain": tile_rep(k_gain[0]),
        "head_bd": jnp.asarray(np.kron(np.eye(NA_HEADS), np.full((HEAD_DIM, HEAD_DIM), 1.0 / HEAD_DIM)), BF16),
        "bias_tbl": _bias_table(rpb[0], dec_seq // GRID_W),
        "sg_w_in": sg_w_in[0].astype(BF16),
        "sg_w_out": sg_w_out[0].astype(BF16),
        "sg_v_gain": sg_v_gain[0].reshape(1, SG_WIDTH),
        "sg_w_s": sg_w_s[0].astype(BF16),
        "sg_b_full": jnp.broadcast_to(sg_b_s[0].T[:, :, None], (CHUNK, SG_GROUPS, SG_GROUP_DIM)).reshape(
            CHUNK, SG_WIDTH),
        "router_w": jnp.pad(router_w, ((0, 0), (0, LANES - N_EXPERTS))),
        "router_b": router_b.reshape(N_EXPERTS, 1),
        "moe_wg": moe_w_gate.astype(BF16),
        "moe_wu": moe_w_up.astype(BF16),
        "moe_wd": moe_w_down.astype(BF16),
    }

    y_prompt, k32, v32 = _trunk(x_prompt, mods4, lambda i: CTX_ROW, None, p, True)
    k_ctx = k32.reshape(batch, 1, seq, NA_HEADS, HEAD_DIM)
    v_ctx = v32.reshape(batch, 1, seq, NA_HEADS, HEAD_DIM)

    tiles_per_batch = dec_seq // TM
    cache = (cache_k[:, 0].reshape(dec_batch, -1, NA_WIDTH).astype(BF16),
             cache_v[:, 0].reshape(dec_batch, -1, NA_WIDTH).astype(BF16))
    y_sample, _, _ = _trunk(x_sample, mods4, lambda i: i // tiles_per_batch, cache, p, False)
    return (y_prompt, y_sample, k_ctx, v_ctx)
```

```python
import functools

import numpy as np
import jax
import jax.numpy as jnp
from jax import lax
from jax.experimental import pallas as pl
from jax.experimental.pallas import tpu as pltpu

F32 = jnp.float32
BF16 = jnp.bfloat16

D_MODEL = 1024
GRID_W = 64
HEAD_DIM = 64
NA_WIDTH = 512
NA_HEADS = 8
WIN_ROWS = 8
WIN_COLS = 16
FN_WIDTH = 512
FN_GROUP_DIM = 64
SG_WIDTH = 1024
SG_GROUPS = 4
SG_GROUP_DIM = 256
CHUNK = 128
N_EXPERTS = 16
N_EXPERT_GROUPS = 4
EXPERTS_PER_GROUP = 4
EXPERT_FF = 256
EPS = 1e-6
NEG_INF = -1e30

LANES = 128
TM = 512
MOE_CHUNK = 160
ROW_GROUP = 4
GROUP_WIN_ROWS = 12
ROW_PITCH = 72
N_COND_ROWS = 8
CTX_ROW = 4
ADA_TN = 1536
FFT_UNROLL = 4
FFT_CB = 256
VMEM_LIMIT = 56 * 1024 * 1024


def _rms_mod(x, g, shift, scale):
    ms = jnp.mean(x * x, axis=-1, keepdims=True)
    y = x * lax.rsqrt(ms + EPS) * g
    return y * (1.0 + scale) + shift


def _mod_chunk(m_ref, i):
    return m_ref[:, i * D_MODEL:(i + 1) * D_MODEL]


def _ada_kernel(c_ref, w_ref, b_ref, o_ref):
    s = jax.nn.silu(c_ref[...]).astype(BF16)
    o_ref[...] = jnp.dot(s, w_ref[...].astype(BF16), preferred_element_type=F32) + b_ref[...]


def _ada_table(cond, ada_w, ada_b):
    depth = ada_w.shape[0]
    n_out = ada_w.shape[2]
    return pl.pallas_call(
        _ada_kernel,
        grid=(depth, n_out // ADA_TN),
        in_specs=[
            pl.BlockSpec((N_COND_ROWS, D_MODEL), lambda l, j: (0, 0)),
            pl.BlockSpec((None, D_MODEL, ADA_TN), lambda l, j: (l, 0, j)),
            pl.BlockSpec((None, 1, ADA_TN), lambda l, j: (l, 0, j)),
        ],
        out_specs=pl.BlockSpec((None, N_COND_ROWS, ADA_TN), lambda l, j: (l, 0, j)),
        out_shape=jax.ShapeDtypeStruct((depth, N_COND_ROWS, n_out), F32),
        name="ada_table",
    )(cond, ada_w, ada_b.reshape(depth, 1, n_out))


def _head_norm(t, bd_ref, gain):
    ms = jnp.dot((t * t).astype(BF16), bd_ref[...], preferred_element_type=F32)
    return t * lax.rsqrt(ms + EPS) * gain


def _inproj_kernel(x_ref, m_ref, g_ref, w_ref, qg_ref, kg_ref, bd_ref, *out_refs, is_ctx):
    h = _rms_mod(x_ref[...], g_ref[...], _mod_chunk(m_ref, 0), _mod_chunk(m_ref, 1)).astype(BF16)
    proj = jnp.dot(h, w_ref[...], preferred_element_type=F32)
    q = _head_norm(proj[:, 0:NA_WIDTH], bd_ref, qg_ref[...]) * (HEAD_DIM ** -0.5)
    k = _head_norm(proj[:, NA_WIDTH:2 * NA_WIDTH], bd_ref, kg_ref[...])
    v = proj[:, 2 * NA_WIDTH:3 * NA_WIDTH]
    u = proj[:, 3 * NA_WIDTH:]
    if is_ctx:
        q_ref, k_ref, v_ref, u_ref, k32_ref, v32_ref = out_refs
        k32_ref[...] = k
        v32_ref[...] = v
        u_ref[...] = u.astype(BF16)
    else:
        q_ref, k_ref, v_ref, u_ref = out_refs
        zeros = jnp.zeros((ROW_PITCH - GRID_W, LANES), F32)
        for r in range(TM // GRID_W):
            for j in range(FN_WIDTH // LANES):
                u_ref[j, r * ROW_PITCH:r * ROW_PITCH + GRID_W, :] = (
                    u[r * GRID_W:(r + 1) * GRID_W, j * LANES:(j + 1) * LANES])
                u_ref[j, r * ROW_PITCH + GRID_W:(r + 1) * ROW_PITCH, :] = zeros
    q_ref[...] = q.astype(BF16)
    k_ref[...] = k.astype(BF16)
    v_ref[...] = v.astype(BF16)


def _mods_spec(layer, row_of_tile):
    return pl.BlockSpec((None, None, 1, 6 * D_MODEL), lambda i: (layer, row_of_tile(i), 0, 0))


def _inproj(x2, mods4, row_of_tile, g, w_in, qg, kg, bd, is_ctx, batch):
    n = x2.shape[0]
    tile = lambda i: (i, 0)
    const = lambda i: (0, 0)
    bf = lambda: jax.ShapeDtypeStruct((n, NA_WIDTH), BF16)
    spec512 = pl.BlockSpec((TM, NA_WIDTH), tile)
    out_shape = [bf(), bf(), bf()]
    out_specs = [spec512, spec512, spec512]
    if is_ctx:
        out_shape += [bf(), jax.ShapeDtypeStruct((n, NA_WIDTH), F32), jax.ShapeDtypeStruct((n, NA_WIDTH), F32)]
        out_specs += [spec512, spec512, spec512]
    else:
        rows = n // batch // GRID_W
        tiles_per_batch = n // batch // TM
        pad_tile = TM // GRID_W * ROW_PITCH
        out_shape += [jax.ShapeDtypeStruct((batch, FN_WIDTH // LANES, rows * ROW_PITCH, LANES), F32)]
        out_specs += [pl.BlockSpec((None, FN_WIDTH // LANES, pad_tile, LANES),
                                   lambda i: (i // tiles_per_batch, 0, i % tiles_per_batch, 0))]
    return pl.pallas_call(
        functools.partial(_inproj_kernel, is_ctx=is_ctx),
        grid=(n // TM,),
        in_specs=[
            pl.BlockSpec((TM, D_MODEL), tile),
            _mods_spec(0, row_of_tile),
            pl.BlockSpec((1, D_MODEL), const),
            pl.BlockSpec((D_MODEL, 4 * NA_WIDTH), const),
            pl.BlockSpec((1, NA_WIDTH), const),
            pl.BlockSpec((1, NA_WIDTH), const),
            pl.BlockSpec((NA_WIDTH, NA_WIDTH), const),
        ],
        out_specs=out_specs,
        out_shape=out_shape,
        name="inproj_ctx" if is_ctx else "inproj_lat",
    )(x2, mods4, g, w_in, qg, kg, bd)


def _softmax_pv(parts):
    m = parts[0][0].max(axis=-1, keepdims=True)
    for s, _ in parts[1:]:
        m = jnp.maximum(m, s.max(axis=-1, keepdims=True))
    l = None
    o = None
    for s, v in parts:
        e = jnp.exp(s - m)
        ls = e.sum(axis=-1, keepdims=True)
        pv = jnp.dot(e.astype(BF16), v, preferred_element_type=F32)
        l = ls if l is None else l + ls
        o = pv if o is None else o + pv
    return o / l


def _nt_dot(a, b):
    return lax.dot_general(a, b, (((1,), (1,)), ((), ())), preferred_element_type=F32)


def _attn_ctx_kernel(q_ref, k_ref, v_ref, o_ref):
    lane = lax.broadcasted_iota(jnp.int32, (q_ref.shape[0], LANES), 1)
    even = lane < HEAD_DIM
    for p in range(NA_HEADS // 2):
        sl = slice(p * LANES, (p + 1) * LANES)
        qp, kp, vp = q_ref[:, sl], k_ref[:, sl], v_ref[:, sl]
        outs = []
        for half in range(2):
            qm = jnp.where(even if half == 0 else ~even, qp, jnp.zeros_like(qp))
            outs.append(_softmax_pv([(_nt_dot(qm, kp), vp)]))
        o_ref[:, sl] = jnp.where(even, outs[0], outs[1]).astype(BF16)


def _attn_ctx(q, k, v, batch):
    n = q.shape[0]
    l = n // batch
    spec = pl.BlockSpec((l, NA_WIDTH), lambda b: (b, 0))
    return pl.pallas_call(
        _attn_ctx_kernel,
        grid=(batch,),
        in_specs=[spec, spec, spec],
        out_specs=spec,
        out_shape=jax.ShapeDtypeStruct((n, NA_WIDTH), BF16),
        name="attn_ctx",
    )(q, k, v)


def _attn_lat_kernel(q_ref, k_ref, v_ref, kc_ref, vc_ref, bias_ref, o_ref, *, rows):
    g = pl.program_id(1)
    base = jnp.clip(ROW_GROUP * g - WIN_ROWS // 2, 0, rows - GROUP_WIN_ROWS)
    start = pl.multiple_of(base * GRID_W, GRID_W)
    win = pl.ds(start, GROUP_WIN_ROWS * GRID_W)
    lane = lax.broadcasted_iota(jnp.int32, (ROW_GROUP * GRID_W, LANES), 1)
    even = lane < HEAD_DIM
    for p in range(NA_HEADS // 2):
        sl = slice(p * LANES, (p + 1) * LANES)
        qp = q_ref[:, sl]
        kw, vw = k_ref[win, sl], v_ref[win, sl]
        kc, vc = kc_ref[:, sl], vc_ref[:, sl]
        outs = []
        for half in range(2):
            qm = jnp.where(even if half == 0 else ~even, qp, jnp.zeros_like(qp))
            s_lat = _nt_dot(qm, kw) + bias_ref[2 * p + half]
            s_ctx = _nt_dot(qm, kc)
            outs.append(_softmax_pv([(s_lat, vw), (s_ctx, vc)]))
        o_ref[:, sl] = jnp.where(even, outs[0], outs[1]).astype(BF16)


def _attn_lat(q, k, v, kc, vc, bias_tbl, batch):
    n = q.shape[0]
    l = n // batch
    rows = l // GRID_W
    groups = rows // ROW_GROUP
    q3, k3, v3 = (t.reshape(batch, l, NA_WIDTH) for t in (q, k, v))
    full = pl.BlockSpec((None, l, NA_WIDTH), lambda b, g: (b, 0, 0))
    ctx = pl.BlockSpec((None, kc.shape[1], NA_WIDTH), lambda b, g: (b, 0, 0))
    qspec = pl.BlockSpec((None, ROW_GROUP * GRID_W, NA_WIDTH), lambda b, g: (b, g, 0))
    bias_class = lambda b, g: (jnp.where(g == 0, 0, jnp.where(g == groups - 1, 2, 1)), 0, 0, 0)
    out = pl.pallas_call(
        functools.partial(_attn_lat_kernel, rows=rows),
        grid=(batch, groups),
        in_specs=[
            qspec, full, full, ctx, ctx,
            pl.BlockSpec((None, NA_HEADS, ROW_GROUP * GRID_W, GROUP_WIN_ROWS * GRID_W), bias_class),
        ],
        out_specs=qspec,
        out_shape=jax.ShapeDtypeStruct((batch, l, NA_WIDTH), BF16),
        compiler_params=pltpu.CompilerParams(vmem_limit_bytes=VMEM_LIMIT),
        name="attn_lat",
    )(q3, k3, v3, kc, vc, bias_tbl)
    return out.reshape(n, NA_WIDTH)


def _bias_table(rpb_i, rows):
    groups = rows // ROW_GROUP
    assert rows % ROW_GROUP == 0 and rows >= GROUP_WIN_ROWS and groups >= 3
    col = np.arange(GRID_W)
    dc = np.clip(col[None, :] - col[:, None] + (WIN_COLS - 1), 0, 2 * WIN_COLS - 2)
    col_start = np.clip(col - WIN_COLS // 2, 0, GRID_W - WIN_COLS)
    col_valid = (col[None, :] >= col_start[:, None]) & (col[None, :] < col_start[:, None] + WIN_COLS)
    col_onehot = (dc[:, :, None] == np.arange(2 * WIN_COLS - 1)).astype(np.float32)
    row_onehot = np.zeros((3, ROW_GROUP, GROUP_WIN_ROWS, 2 * WIN_ROWS - 1), np.float32)
    row_valid = np.zeros((3, ROW_GROUP, GROUP_WIN_ROWS), bool)
    for cls, g in enumerate((0, 1, groups - 1)):
        base = np.clip(ROW_GROUP * g - WIN_ROWS // 2, 0, rows - GROUP_WIN_ROWS)
        for j in range(ROW_GROUP):
            r = ROW_GROUP * g + j
            row_start = np.clip(r - WIN_ROWS // 2, 0, rows - WIN_ROWS)
            for u in range(GROUP_WIN_ROWS):
                if row_start <= base + u < row_start + WIN_ROWS:
                    row_onehot[cls, j, u, base + u - r + WIN_ROWS - 1] = 1.0
                    row_valid[cls, j, u] = True
    t = jnp.einsum("hrc,xjur,qkc->xhjquk", rpb_i.astype(F32), row_onehot, col_onehot,
                   precision=lax.Precision.HIGHEST)
    valid = row_valid[:, None, :, None, :, None] & col_valid[None, None, None, :, None, :]
    t = jnp.where(valid, t, NEG_INF)
    return t.reshape(3, NA_HEADS, ROW_GROUP * GRID_W, GROUP_WIN_ROWS * GRID_W)


def _bf16_table(a):
    return jnp.asarray(a, F32).astype(BF16)


def _dft_cos_sin(n):
    idx = np.arange(n)
    ang = 2.0 * np.pi * ((idx[:, None] * idx[None, :]) % n) / n
    return np.cos(ang), np.sin(ang)


def _channel_dft(width, scale):
    c, s = _dft_cos_sin(FN_GROUP_DIM)
    eye = np.eye(width // FN_GROUP_DIM)
    return np.concatenate([np.kron(eye, c), np.kron(eye, s)], axis=0) * scale


def _fft_ctx_kernel(u_ref, fc_ref, cs_ref, o_ref):
    l = u_ref.shape[0]
    x = jnp.dot(fc_ref[...], u_ref[...], preferred_element_type=F32)
    xri = jnp.concatenate([x[:l], x[l:]], axis=1).astype(BF16)
    o_ref[...] = jnp.dot(xri, cs_ref[...], preferred_element_type=F32).astype(BF16)


def _fft_ctx(u, batch):
    n = u.shape[0]
    l = n // batch
    c, s = _dft_cos_sin(l)
    fc = _bf16_table(np.concatenate([c, -s], axis=0))
    cs = _bf16_table(_channel_dft(FN_WIDTH, (l * FN_GROUP_DIM) ** -0.5))
    spec = pl.BlockSpec((l, FN_WIDTH), lambda b: (b, 0))
    return pl.pallas_call(
        _fft_ctx_kernel,
        grid=(batch,),
        in_specs=[spec, pl.BlockSpec(fc.shape, lambda b: (0, 0)), pl.BlockSpec(cs.shape, lambda b: (0, 0))],
        out_specs=spec,
        out_shape=jax.ShapeDtypeStruct((n, FN_WIDTH), BF16),
        name="fourier_ctx",
    )(u, fc, cs)


def _fft_lat_kernel(u_ref, f1_ref, twr_ref, twi_ref, f2_ref, cs_ref, o_ref, z_ref, x2_ref):
    slabs = FFT_CB // LANES
    n1 = GRID_W
    x2_ref[...] = jnp.zeros(x2_ref.shape, F32)

    def stage1(n2, carry):
        u = jnp.concatenate([u_ref[j, pl.ds(n2, n1, stride=ROW_PITCH), :] for j in range(slabs)], axis=1)
        y = jnp.dot(f1_ref[...], u.astype(BF16), preferred_element_type=F32)
        yr, yi = y[:n1], y[n1:]
        tr = jnp.concatenate([twr_ref[n2]] * slabs, axis=1)
        ti = jnp.concatenate([twi_ref[n2]] * slabs, axis=1)
        zr = yr * tr - yi * ti
        zi = yr * ti + yi * tr
        for j in range(slabs):
            sl = slice(j * LANES, (j + 1) * LANES)
            z_ref[0, j, pl.ds(n2, n1, stride=ROW_PITCH), :] = zr[:, sl]
            z_ref[1, j, pl.ds(n2, n1, stride=ROW_PITCH), :] = zi[:, sl]
        return carry

    lax.fori_loop(0, GRID_W, stage1, 0, unroll=FFT_UNROLL)

    def stage2(k1, carry):
        base = pl.multiple_of(k1 * ROW_PITCH, 8)
        zr = jnp.concatenate([z_ref[0, j, pl.ds(base, GRID_W), :] for j in range(slabs)], axis=1)
        zi = jnp.concatenate([z_ref[1, j, pl.ds(base, GRID_W), :] for j in range(slabs)], axis=1)
        zz = jnp.concatenate([zr, zi], axis=0).astype(BF16)
        x = jnp.dot(f2_ref[...], zz, preferred_element_type=F32)
        for j in range(slabs):
            sl = slice(j * LANES, (j + 1) * LANES)
            x2_ref[j, pl.ds(k1, GRID_W, stride=ROW_PITCH), :] = x[:GRID_W, sl]
            x2_ref[slabs + j, pl.ds(k1, GRID_W, stride=ROW_PITCH), :] = x[GRID_W:, sl]
        return carry

    lax.fori_loop(0, n1, stage2, 0, unroll=FFT_UNROLL)

    chunk = 8 * ROW_PITCH
    for c in range(GRID_W * ROW_PITCH // chunk):
        rs = slice(c * chunk, (c + 1) * chunk)
        xri = jnp.concatenate([x2_ref[s, rs, :] for s in range(2 * slabs)], axis=1).astype(BF16)
        o_ref[rs, :] = jnp.dot(xri, cs_ref[...], preferred_element_type=F32)


def _fft_lat(u_pad, batch):
    l = GRID_W * GRID_W
    rows_pad = GRID_W * ROW_PITCH
    slabs = FFT_CB // LANES
    c, s = _dft_cos_sin(GRID_W)
    f1 = _bf16_table(np.concatenate([c, -s], axis=0))
    f2 = _bf16_table(np.block([[c, s], [-s, c]]))
    n2k1 = (np.arange(GRID_W)[:, None] * np.arange(GRID_W)[None, :]) % l
    ang = 2.0 * np.pi * n2k1 / l
    twr = jnp.broadcast_to(jnp.asarray(np.cos(ang), F32)[:, :, None], (GRID_W, GRID_W, LANES))
    twi = jnp.broadcast_to(jnp.asarray(-np.sin(ang), F32)[:, :, None], (GRID_W, GRID_W, LANES))
    cs = _bf16_table(_channel_dft(FFT_CB, (l * FN_GROUP_DIM) ** -0.5))
    const2 = lambda b, cb: (0, 0)
    const3 = lambda b, cb: (0, 0, 0)
    return pl.pallas_call(
        _fft_lat_kernel,
        grid=(batch, FN_WIDTH // FFT_CB),
        in_specs=[
            pl.BlockSpec((None, slabs, rows_pad, LANES), lambda b, cb: (b, cb, 0, 0)),
            pl.BlockSpec(f1.shape, const2),
            pl.BlockSpec(twr.shape, const3),
            pl.BlockSpec(twi.shape, const3),
            pl.BlockSpec(f2.shape, const2),
            pl.BlockSpec(cs.shape, const2),
        ],
        out_specs=pl.BlockSpec((None, rows_pad, FFT_CB), lambda b, cb: (b, 0, cb)),
        out_shape=jax.ShapeDtypeStruct((batch, rows_pad, FN_WIDTH), F32),
        scratch_shapes=[
            pltpu.VMEM((2, slabs, rows_pad, LANES), F32),
            pltpu.VMEM((2 * slabs, rows_pad, LANES), F32),
        ],
        compiler_params=pltpu.CompilerParams(vmem_limit_bytes=VMEM_LIMIT),
        name="fourier_lat",
    )(u_pad, f1, twr, twi, f2, cs)


def _outproj_kernel(x_ref, a_ref, f_ref, m_ref, w_ref, o_ref, *, is_ctx):
    if is_ctx:
        f = f_ref[...]
    else:
        f = jnp.concatenate(
            [f_ref[r * ROW_PITCH:r * ROW_PITCH + GRID_W, :] for r in range(TM // GRID_W)], axis=0).astype(BF16)
    af = jnp.concatenate([a_ref[...], f], axis=1)
    mix = jnp.dot(af, w_ref[...], preferred_element_type=F32)
    o_ref[...] = x_ref[...] + _mod_chunk(m_ref, 2) * mix


def _outproj(x2, a, f, mods4, row_of_tile, w_out, is_ctx, batch):
    n = x2.shape[0]
    tile = lambda i: (i, 0)
    if is_ctx:
        fspec = pl.BlockSpec((TM, FN_WIDTH), tile)
    else:
        tiles_per_batch = n // batch // TM
        fspec = pl.BlockSpec((None, TM // GRID_W * ROW_PITCH, FN_WIDTH),
                             lambda i: (i // tiles_per_batch, i % tiles_per_batch, 0))
    return pl.pallas_call(
        functools.partial(_outproj_kernel, is_ctx=is_ctx),
        grid=(n // TM,),
        in_specs=[
            pl.BlockSpec((TM, D_MODEL), tile),
            pl.BlockSpec((TM, NA_WIDTH), tile),
            fspec,
            _mods_spec(0, row_of_tile),
            pl.BlockSpec((D_MODEL, D_MODEL), lambda i: (0, 0)),
        ],
        out_specs=pl.BlockSpec((TM, D_MODEL), tile),
        out_shape=jax.ShapeDtypeStruct((n, D_MODEL), F32),
        name="outproj_ctx" if is_ctx else "outproj_lat",
    )(x2, a, f, mods4, w_out)


def _sg_kernel(x_ref, m_ref, g_ref, win_ref, vg_ref, ws_ref, bs_ref, wout_ref, o_ref):
    x = x_ref[...]
    h = _rms_mod(x, g_ref[...], _mod_chunk(m_ref, 0), _mod_chunk(m_ref, 1)).astype(BF16)
    z = jax.nn.gelu(jnp.dot(h, win_ref[...], preferred_element_type=F32))
    gated = []
    for g in range(SG_GROUPS):
        gs = slice(g * SG_GROUP_DIM, (g + 1) * SG_GROUP_DIM)
        u = z[:, gs]
        v = z[:, SG_WIDTH + g * SG_GROUP_DIM:SG_WIDTH + (g + 1) * SG_GROUP_DIM]
        ms = jnp.mean(v * v, axis=-1, keepdims=True)
        vn = (v * lax.rsqrt(ms + EPS) * vg_ref[:, gs]).astype(BF16)
        w = ws_ref[g]
        s = jnp.concatenate(
            [jnp.dot(w, vn[c * CHUNK:(c + 1) * CHUNK], preferred_element_type=F32) for c in range(TM // CHUNK)],
            axis=0)
        bias = jnp.concatenate([bs_ref[:, gs]] * (TM // CHUNK), axis=0)
        gated.append((u * (s + bias)).astype(BF16))
    mix = jnp.dot(jnp.concatenate(gated, axis=1), wout_ref[...], preferred_element_type=F32)
    o_ref[...] = x + _mod_chunk(m_ref, 2) * mix


def _spatial_gating(x2, mods4, row_of_tile, g, w_in, v_gain, w_s, b_full, w_out):
    n = x2.shape[0]
    tile = lambda i: (i, 0)
    const = lambda i: (0, 0)
    return pl.pallas_call(
        _sg_kernel,
        grid=(n // TM,),
        in_specs=[
            pl.BlockSpec((TM, D_MODEL), tile),
            _mods_spec(1, row_of_tile),
            pl.BlockSpec((1, D_MODEL), const),
            pl.BlockSpec((D_MODEL, 2 * SG_WIDTH), const),
            pl.BlockSpec((1, SG_WIDTH), const),
            pl.BlockSpec((SG_GROUPS, CHUNK, CHUNK), lambda i: (0, 0, 0)),
            pl.BlockSpec((CHUNK, SG_WIDTH), const),
            pl.BlockSpec((SG_WIDTH, D_MODEL), const),
        ],
        out_specs=pl.BlockSpec((TM, D_MODEL), tile),
        out_shape=jax.ShapeDtypeStruct((n, D_MODEL), F32),
        name="spatial_gating",
    )(x2, mods4, g, w_in, v_gain, w_s, b_full, w_out)


def _first_index_of(mask, idx, size):
    return jnp.min(jnp.where(mask, idx, size), axis=0, keepdims=True)


def _route(h, h_hi, rw_ref, rb_ref):
    h_lo = (h - h_hi.astype(F32)).astype(BF16)
    rw = rw_ref[...]
    rw_hi = rw.astype(BF16)
    rw_lo = (rw - rw_hi.astype(F32)).astype(BF16)
    logits = (jnp.dot(h_hi, rw_hi, preferred_element_type=F32)
              + jnp.dot(h_lo, rw_hi, preferred_element_type=F32)
              + jnp.dot(h_hi, rw_lo, preferred_element_type=F32))
    logits = logits.T[:N_EXPERTS]
    scores = jax.nn.sigmoid(logits)
    sel = scores + rb_ref[...]
    eg = EXPERTS_PER_GROUP
    ri = lax.broadcasted_iota(jnp.int32, (eg, TM), 0)
    firsts, seconds, group_scores = [], [], []
    for g in range(N_EXPERT_GROUPS):
        a = sel[g * eg:(g + 1) * eg]
        m1 = a.max(axis=0, keepdims=True)
        i1 = _first_index_of(a == m1, ri, eg)
        rest = jnp.where(ri == i1, -jnp.inf, a)
        m2 = rest.max(axis=0, keepdims=True)
        i2 = _first_index_of(rest == m2, ri, eg)
        firsts.append(i1)
        seconds.append(i2)
        group_scores.append(m1 + m2)
    gs = jnp.concatenate(group_scores, axis=0)
    gi = lax.broadcasted_iota(jnp.int32, gs.shape, 0)
    g_best = _first_index_of(gs == gs.max(axis=0, keepdims=True), gi, N_EXPERT_GROUPS)
    picked = []
    for g in range(N_EXPERT_GROUPS):
        chosen = (g_best == g) & ((ri == firsts[g]) | (ri == seconds[g]))
        picked.append(jnp.where(chosen, scores[g * eg:(g + 1) * eg], 0.0))
    w = jnp.concatenate(picked, axis=0)
    return w / w.sum(axis=0, keepdims=True), g_best


def _split_bf16(t):
    hi = t.astype(BF16)
    return hi, (t - hi.astype(F32)).astype(BF16)


def _tn_dot(a, b):
    return lax.dot_general(a, b, (((0,), (0,)), ((), ())), preferred_element_type=F32)


def _moe_kernel(x_ref, m_ref, g_ref, rw_ref, rb_ref, tri_ref, wg_ref, wu_ref, wd_ref, o_ref, y_ref):
    x = x_ref[...]
    h = _rms_mod(x, g_ref[...], _mod_chunk(m_ref, 3), _mod_chunk(m_ref, 4))
    h_hi = h.astype(BF16)
    w, g_best = _route(h, h_hi, rw_ref, rb_ref)
    wt = jnp.concatenate([w, jnp.zeros((LANES - N_EXPERTS, TM), F32)], axis=0)
    gates_hi, gates_lo = _split_bf16(wt.T)

    gi = lax.broadcasted_iota(jnp.int32, (2 * N_EXPERT_GROUPS, TM), 0)
    onehot = jnp.where(gi == g_best, 1.0, 0.0)
    prefix = jnp.dot(onehot.astype(BF16), tri_ref[...], preferred_element_type=F32)
    rank = jnp.sum(onehot * prefix, axis=0, keepdims=True)
    slot = lax.broadcasted_iota(jnp.int32, (MOE_CHUNK, TM), 0).astype(F32)

    y_ref[...] = jnp.zeros(y_ref.shape, F32)
    eg = EXPERTS_PER_GROUP
    for g in range(N_EXPERT_GROUPS):
        count = jnp.sum(onehot[g:g + 1]).astype(jnp.int32)

        def chunk(j, carry, g=g):
            key = jnp.where(g_best == g, rank - (j * MOE_CHUNK).astype(F32), -1.0)
            perm = jnp.where(slot == key, 1.0, 0.0).astype(BF16)
            hs = jnp.dot(perm, h_hi, preferred_element_type=F32).astype(BF16)
            gs = (jnp.dot(perm, gates_hi, preferred_element_type=F32)
                  + jnp.dot(perm, gates_lo, preferred_element_type=F32))
            y = jnp.zeros((MOE_CHUNK, D_MODEL), F32)
            for e in range(g * eg, (g + 1) * eg):
                a = jnp.dot(hs, wg_ref[e], preferred_element_type=F32)
                b = jnp.dot(hs, wu_ref[e], preferred_element_type=F32)
                hid = jax.nn.silu(a) * b * gs[:, e:e + 1]
                y = y + jnp.dot(hid.astype(BF16), wd_ref[e], preferred_element_type=F32)
            y_hi, y_lo = _split_bf16(y)
            y_ref[...] += _tn_dot(perm, y_hi) + _tn_dot(perm, y_lo)
            return carry

        lax.fori_loop(0, (count + MOE_CHUNK - 1) // MOE_CHUNK, chunk, 0)
    o_ref[...] = x + _mod_chunk(m_ref, 5) * y_ref[...]


def _moe(x2, mods4, layer, row_of_tile, g, rw, rb, wg, wu, wd):
    n = x2.shape[0]
    tile = lambda i: (i, 0)
    const = lambda i: (0, 0)
    resident = lambda shape: pl.BlockSpec(shape, lambda i: (0, 0, 0), pipeline_mode=pl.Buffered(1))
    tri = jnp.asarray(np.triu(np.ones((TM, TM), np.float32), k=1), BF16)
    return pl.pallas_call(
        _moe_kernel,
        grid=(n // TM,),
        in_specs=[
            pl.BlockSpec((TM, D_MODEL), tile),
            _mods_spec(layer, row_of_tile),
            pl.BlockSpec((1, D_MODEL), const),
            pl.BlockSpec((D_MODEL, LANES), const),
            pl.BlockSpec((N_EXPERTS, 1), const),
            pl.BlockSpec((TM, TM), const),
            resident(wg.shape), resident(wu.shape), resident(wd.shape),
        ],
        out_specs=pl.BlockSpec((TM, D_MODEL), tile),
        out_shape=jax.ShapeDtypeStruct((n, D_MODEL), F32),
        scratch_shapes=[pltpu.VMEM((TM, D_MODEL), F32)],
        compiler_params=pltpu.CompilerParams(vmem_limit_bytes=VMEM_LIMIT),
        name="moe",
    )(x2, mods4, g, rw, rb, tri, wg, wu, wd)


def _trunk(x, mods4, row_of_tile, cache, p, is_ctx):
    batch, l, _ = x.shape
    x2 = x.reshape(batch * l, D_MODEL)

    outs = _inproj(x2, mods4, row_of_tile, p["norm1_g"][0], p["na_w_in"], p["q_gain"], p["k_gain"], p["head_bd"],
                   is_ctx, batch)
    if is_ctx:
        q, k, v, u, k32, v32 = outs
        a = _attn_ctx(q, k, v, batch)
        f = _fft_ctx(u, batch)
    else:
        q, k, v, u_pad = outs
        a = _attn_lat(q, k, v, cache[0], cache[1], p["bias_tbl"], batch)
        f = _fft_lat(u_pad, batch)
        k32 = v32 = None
    x2 = _outproj(x2, a, f, mods4, row_of_tile, p["na_w_out"], is_ctx, batch)
    x2 = _moe(x2, mods4, 0, row_of_tile, p["norm2_g"][0], p["router_w"], p["router_b"],
              p["moe_wg"][0], p["moe_wu"][0], p["moe_wd"][0])

    x2 = _spatial_gating(x2, mods4, row_of_tile, p["norm1_g"][1], p["sg_w_in"], p["sg_v_gain"], p["sg_w_s"],
                         p["sg_b_full"], p["sg_w_out"])
    x2 = _moe(x2, mods4, 1, row_of_tile, p["norm2_g"][1], p["router_w"], p["router_b"],
              p["moe_wg"][1], p["moe_wu"][1], p["moe_wd"][1])
    return x2.reshape(batch, l, D_MODEL), k32, v32


def kernel(x_prompt, x_sample, cache_k, cache_v, c, c_ctx, ada_w, ada_b, norm1_g, norm2_g, na_w_in, na_w_out,
           q_gain, k_gain, rpb, sg_w_in, sg_w_out, sg_v_gain, sg_w_s, sg_b_s, router_w, router_b,
           moe_w_gate, moe_w_up, moe_w_down):
    batch, seq, _ = x_prompt.shape
    dec_batch, dec_seq, _ = x_sample.shape
    assert dec_batch <= CTX_ROW and dec_seq == GRID_W * GRID_W and (batch * seq) % TM == 0
    assert ada_w.shape[0] == 2 and na_w_in.shape[0] == 1 and sg_w_in.shape[0] == 1

    cond = jnp.zeros((N_COND_ROWS, D_MODEL), F32).at[:dec_batch].set(c).at[CTX_ROW].set(c_ctx)
    mods = _ada_table(cond, ada_w, ada_b)
    mods4 = mods.reshape(mods.shape[0], N_COND_ROWS, 1, mods.shape[2])

    tile_rep = lambda t: jnp.tile(t, NA_HEADS).reshape(1, NA_WIDTH)
    p = {
        "norm1_g": norm1_g.reshape(-1, 1, D_MODEL),
        "norm2_g": norm2_g.reshape(-1, 1, D_MODEL),
        "na_w_in": na_w_in[0].astype(BF16),
        "na_w_out": na_w_out[0].astype(BF16),
        "q_gain": tile_rep(q_gain[0]),
        "k_gain": tile_rep(k_gain[0]),
        "head_bd": jnp.asarray(np.kron(np.eye(NA_HEADS), np.full((HEAD_DIM, HEAD_DIM), 1.0 / HEAD_DIM)), BF16),
        "bias_tbl": _bias_table(rpb[0], dec_seq // GRID_W),
        "sg_w_in": sg_w_in[0].astype(BF16),
        "sg_w_out": sg_w_out[0].astype(BF16),
        "sg_v_gain": sg_v_gain[0].reshape(1, SG_WIDTH),
        "sg_w_s": sg_w_s[0].astype(BF16),
        "sg_b_full": jnp.broadcast_to(sg_b_s[0].T[:, :, None], (CHUNK, SG_GROUPS, SG_GROUP_DIM)).reshape(
            CHUNK, SG_WIDTH),
        "router_w": jnp.pad(router_w, ((0, 0), (0, LANES - N_EXPERTS))),
        "router_b": router_b.reshape(N_EXPERTS, 1),
        "moe_wg": moe_w_gate.astype(BF16),
        "moe_wu": moe_w_up.astype(BF16),
        "moe_wd": moe_w_down.astype(BF16),
    }

    y_prompt, k32, v32 = _trunk(x_prompt, mods4, lambda i: CTX_ROW, None, p, True)
    k_ctx = k32.reshape(batch, 1, seq, NA_HEADS, HEAD_DIM)
    v_ctx = v32.reshape(batch, 1, seq, NA_HEADS, HEAD_DIM)

    tiles_per_batch = dec_seq // TM
    cache = (cache_k[:, 0].reshape(dec_batch, -1, NA_WIDTH).astype(BF16),
             cache_v[:, 0].reshape(dec_batch, -1, NA_WIDTH).astype(BF16))
    y_sample, _, _ = _trunk(x_sample, mods4, lambda i: i // tiles_per_batch, cache, p, False)
    return (y_prompt, y_sample, k_ctx, v_ctx)
```

```python
import functools

import numpy as np
import jax
import jax.numpy as jnp
from jax import lax
from jax.experimental import pallas as pl
from jax.experimental.pallas import tpu as pltpu

F32 = jnp.float32
BF16 = jnp.bfloat16

D_MODEL = 1024
GRID_W = 64
HEAD_DIM = 64
NA_WIDTH = 512
NA_HEADS = 8
WIN_ROWS = 8
WIN_COLS = 16
FN_WIDTH = 512
FN_GROUP_DIM = 64
SG_WIDTH = 1024
SG_GROUPS = 4
SG_GROUP_DIM = 256
CHUNK = 128
N_EXPERTS = 16
N_EXPERT_GROUPS = 4
EXPERTS_PER_GROUP = 4
EXPERT_FF = 256
EPS = 1e-6
NEG_INF = -1e30

LANES = 128
TM = 512
MOE_CHUNK = 160
ROW_GROUP = 4
GROUP_WIN_ROWS = 12
ROW_PITCH = 72
N_COND_ROWS = 8
CTX_ROW = 4
ADA_TN = 1536
FFT_UNROLL = 4
FFT_CB = 256
VMEM_LIMIT = 56 * 1024 * 1024


def _rms_mod(x, g, shift, scale):
    ms = jnp.mean(x * x, axis=-1, keepdims=True)
    y = x * lax.rsqrt(ms + EPS) * g
    return y * (1.0 + scale) + shift


def _mod_chunk(m_ref, i):
    return m_ref[:, i * D_MODEL:(i + 1) * D_MODEL]


def _ada_kernel(c_ref, w_ref, b_ref, o_ref):
    s = jax.nn.silu(c_ref[...]).astype(BF16)
    o_ref[...] = jnp.dot(s, w_ref[...].astype(BF16), preferred_element_type=F32) + b_ref[...]


def _ada_table(cond, ada_w, ada_b):
    depth = ada_w.shape[0]
    n_out = ada_w.shape[2]
    return pl.pallas_call(
        _ada_kernel,
        grid=(depth, n_out // ADA_TN),
        in_specs=[
            pl.BlockSpec((N_COND_ROWS, D_MODEL), lambda l, j: (0, 0)),
            pl.BlockSpec((None, D_MODEL, ADA_TN), lambda l, j: (l, 0, j)),
            pl.BlockSpec((None, 1, ADA_TN), lambda l, j: (l, 0, j)),
        ],
        out_specs=pl.BlockSpec((None, N_COND_ROWS, ADA_TN), lambda l, j: (l, 0, j)),
        out_shape=jax.ShapeDtypeStruct((depth, N_COND_ROWS, n_out), F32),
        name="ada_table",
    )(cond, ada_w, ada_b.reshape(depth, 1, n_out))


def _head_norm(t, bd_ref, gain):
    ms = jnp.dot((t * t).astype(BF16), bd_ref[...], preferred_element_type=F32)
    return t * lax.rsqrt(ms + EPS) * gain


def _inproj_kernel(x_ref, m_ref, g_ref, w_ref, qg_ref, kg_ref, bd_ref, *out_refs, is_ctx):
    h = _rms_mod(x_ref[...], g_ref[...], _mod_chunk(m_ref, 0), _mod_chunk(m_ref, 1)).astype(BF16)
    proj = jnp.dot(h, w_ref[...], preferred_element_type=F32)
    q = _head_norm(proj[:, 0:NA_WIDTH], bd_ref, qg_ref[...]) * (HEAD_DIM ** -0.5)
    k = _head_norm(proj[:, NA_WIDTH:2 * NA_WIDTH], bd_ref, kg_ref[...])
    v = proj[:, 2 * NA_WIDTH:3 * NA_WIDTH]
    u = proj[:, 3 * NA_WIDTH:]
    if is_ctx:
        q_ref, k_ref, v_ref, u_ref, k32_ref, v32_ref = out_refs
        k32_ref[...] = k
        v32_ref[...] = v
        u_ref[...] = u.astype(BF16)
    else:
        q_ref, k_ref, v_ref, u_ref = out_refs
        zeros = jnp.zeros((ROW_PITCH - GRID_W, LANES), F32)
        for r in range(TM // GRID_W):
            for j in range(FN_WIDTH // LANES):
                u_ref[j, r * ROW_PITCH:r * ROW_PITCH + GRID_W, :] = (
                    u[r * GRID_W:(r + 1) * GRID_W, j * LANES:(j + 1) * LANES])
                u_ref[j, r * ROW_PITCH + GRID_W:(r + 1) * ROW_PITCH, :] = zeros
    q_ref[...] = q.astype(BF16)
    k_ref[...] = k.astype(BF16)
    v_ref[...] = v.astype(BF16)


def _mods_spec(layer, row_of_tile):
    return pl.BlockSpec((None, None, 1, 6 * D_MODEL), lambda i: (layer, row_of_tile(i), 0, 0))


def _inproj(x2, mods4, row_of_tile, g, w_in, qg, kg, bd, is_ctx, batch):
    n = x2.shape[0]
    tile = lambda i: (i, 0)
    const = lambda i: (0, 0)
    bf = lambda: jax.ShapeDtypeStruct((n, NA_WIDTH), BF16)
    spec512 = pl.BlockSpec((TM, NA_WIDTH), tile)
    out_shape = [bf(), bf(), bf()]
    out_specs = [spec512, spec512, spec512]
    if is_ctx:
        out_shape += [bf(), jax.ShapeDtypeStruct((n, NA_WIDTH), F32), jax.ShapeDtypeStruct((n, NA_WIDTH), F32)]
        out_specs += [spec512, spec512, spec512]
    else:
        rows = n // batch // GRID_W
        tiles_per_batch = n // batch // TM
        pad_tile = TM // GRID_W * ROW_PITCH
        out_shape += [jax.ShapeDtypeStruct((batch, FN_WIDTH // LANES, rows * ROW_PITCH, LANES), F32)]
        out_specs += [pl.BlockSpec((None, FN_WIDTH // LANES, pad_tile, LANES),
                                   lambda i: (i // tiles_per_batch, 0, i % tiles_per_batch, 0))]
    return pl.pallas_call(
        functools.partial(_inproj_kernel, is_ctx=is_ctx),
        grid=(n // TM,),
        in_specs=[
            pl.BlockSpec((TM, D_MODEL), tile),
            _mods_spec(0, row_of_tile),
            pl.BlockSpec((1, D_MODEL), const),
            pl.BlockSpec((D_MODEL, 4 * NA_WIDTH), const),
            pl.BlockSpec((1, NA_WIDTH), const),
            pl.BlockSpec((1, NA_WIDTH), const),
            pl.BlockSpec((NA_WIDTH, NA_WIDTH), const),
        ],
        out_specs=out_specs,
        out_shape=out_shape,
        name="inproj_ctx" if is_ctx else "inproj_lat",
    )(x2, mods4, g, w_in, qg, kg, bd)


def _softmax_pv(parts):
    m = parts[0][0].max(axis=-1, keepdims=True)
    for s, _ in parts[1:]:
        m = jnp.maximum(m, s.max(axis=-1, keepdims=True))
    l = None
    o = None
    for s, v in parts:
        e = jnp.exp(s - m)
        ls = e.sum(axis=-1, keepdims=True)
        pv = jnp.dot(e.astype(BF16), v, preferred_element_type=F32)
        l = ls if l is None else l + ls
        o = pv if o is None else o + pv
    return o / l


def _nt_dot(a, b):
    return lax.dot_general(a, b, (((1,), (1,)), ((), ())), preferred_element_type=F32)


def _attn_ctx_kernel(q_ref, k_ref, v_ref, o_ref):
    lane = lax.broadcasted_iota(jnp.int32, (q_ref.shape[0], LANES), 1)
    even = lane < HEAD_DIM
    for p in range(NA_HEADS // 2):
        sl = slice(p * LANES, (p + 1) * LANES)
        qp, kp, vp = q_ref[:, sl], k_ref[:, sl], v_ref[:, sl]
        outs = []
        for half in range(2):
            qm = jnp.where(even if half == 0 else ~even, qp, jnp.zeros_like(qp))
            outs.append(_softmax_pv([(_nt_dot(qm, kp), vp)]))
        o_ref[:, sl] = jnp.where(even, outs[0], outs[1]).astype(BF16)


def _attn_ctx(q, k, v, batch):
    n = q.shape[0]
    l = n // batch
    spec = pl.BlockSpec((l, NA_WIDTH), lambda b: (b, 0))
    return pl.pallas_call(
        _attn_ctx_kernel,
        grid=(batch,),
        in_specs=[spec, spec, spec],
        out_specs=spec,
        out_shape=jax.ShapeDtypeStruct((n, NA_WIDTH), BF16),
        name="attn_ctx",
    )(q, k, v)


def _attn_lat_kernel(q_ref, k_ref, v_ref, kc_ref, vc_ref, bias_ref, o_ref, *, rows):
    g = pl.program_id(1)
    base = jnp.clip(ROW_GROUP * g - WIN_ROWS // 2, 0, rows - GROUP_WIN_ROWS)
    start = pl.multiple_of(base * GRID_W, GRID_W)
    win = pl.ds(start, GROUP_WIN_ROWS * GRID_W)
    lane = lax.broadcasted_iota(jnp.int32, (ROW_GROUP * GRID_W, LANES), 1)
    even = lane < HEAD_DIM
    for p in range(NA_HEADS // 2):
        sl = slice(p * LANES, (p + 1) * LANES)
        qp = q_ref[:, sl]
        kw, vw = k_ref[win, sl], v_ref[win, sl]
        kc, vc = kc_ref[:, sl], vc_ref[:, sl]
        outs = []
        for half in range(2):
            qm = jnp.where(even if half == 0 else ~even, qp, jnp.zeros_like(qp))
            s_lat = _nt_dot(qm, kw) + bias_ref[2 * p + half]
            s_ctx = _nt_dot(qm, kc)
            outs.append(_softmax_pv([(s_lat, vw), (s_ctx, vc)]))
        o_ref[:, sl] = jnp.where(even, outs[0], outs[1]).astype(BF16)


def _attn_lat(q, k, v, kc, vc, bias_tbl, batch):
    n = q.shape[0]
    l = n // batch
    rows = l // GRID_W
    groups = rows // ROW_GROUP
    q3, k3, v3 = (t.reshape(batch, l, NA_WIDTH) for t in (q, k, v))
    full = pl.BlockSpec((None, l, NA_WIDTH), lambda b, g: (b, 0, 0))
    ctx = pl.BlockSpec((None, kc.shape[1], NA_WIDTH), lambda b, g: (b, 0, 0))
    qspec = pl.BlockSpec((None, ROW_GROUP * GRID_W, NA_WIDTH), lambda b, g: (b, g, 0))
    bias_class = lambda b, g: (jnp.where(g == 0, 0, jnp.where(g == groups - 1, 2, 1)), 0, 0, 0)
    out = pl.pallas_call(
        functools.partial(_attn_lat_kernel, rows=rows),
        grid=(batch, groups),
        in_specs=[
            qspec, full, full, ctx, ctx,
            pl.BlockSpec((None, NA_HEADS, ROW_GROUP * GRID_W, GROUP_WIN_ROWS * GRID_W), bias_class),
        ],
        out_specs=qspec,
        out_shape=jax.ShapeDtypeStruct((batch, l, NA_WIDTH), BF16),
        compiler_params=pltpu.CompilerParams(vmem_limit_bytes=VMEM_LIMIT),
        name="attn_lat",
    )(q3, k3, v3, kc, vc, bias_tbl)
    return out.reshape(n, NA_WIDTH)


def _bias_table(rpb_i, rows):
    groups = rows // ROW_GROUP
    assert rows % ROW_GROUP == 0 and rows >= GROUP_WIN_ROWS and groups >= 3
    col = np.arange(GRID_W)
    dc = np.clip(col[None, :] - col[:, None] + (WIN_COLS - 1), 0, 2 * WIN_COLS - 2)
    col_start = np.clip(col - WIN_COLS // 2, 0, GRID_W - WIN_COLS)
    col_valid = (col[None, :] >= col_start[:, None]) & (col[None, :] < col_start[:, None] + WIN_COLS)
    col_onehot = (dc[:, :, None] == np.arange(2 * WIN_COLS - 1)).astype(np.float32)
    row_onehot = np.zeros((3, ROW_GROUP, GROUP_WIN_ROWS, 2 * WIN_ROWS - 1), np.float32)
    row_valid = np.zeros((3, ROW_GROUP, GROUP_WIN_ROWS), bool)
    for cls, g in enumerate((0, 1, groups - 1)):
        base = np.clip(ROW_GROUP * g - WIN_ROWS // 2, 0, rows - GROUP_WIN_ROWS)
        for j in range(ROW_GROUP):
            r = ROW_GROUP * g + j
            row_start = np.clip(r - WIN_ROWS // 2, 0, rows - WIN_ROWS)
            for u in range(GROUP_WIN_ROWS):
                if row_start <= base + u < row_start + WIN_ROWS:
                    row_onehot[cls, j, u, base + u - r + WIN_ROWS - 1] = 1.0
                    row_valid[cls, j, u] = True
    t = jnp.einsum("hrc,xjur,qkc->xhjquk", rpb_i.astype(F32), row_onehot, col_onehot,
                   precision=lax.Precision.HIGHEST)
    valid = row_valid[:, None, :, None, :, None] & col_valid[None, None, None, :, None, :]
    t = jnp.where(valid, t, NEG_INF)
    return t.reshape(3, NA_HEADS, ROW_GROUP * GRID_W, GROUP_WIN_ROWS * GRID_W)


def _bf16_table(a):
    return jnp.asarray(a, F32).astype(BF16)


def _dft_cos_sin(n):
    idx = np.arange(n)
    ang = 2.0 * np.pi * ((idx[:, None] * idx[None, :]) % n) / n
    return np.cos(ang), np.sin(ang)


def _channel_dft(width, scale):
    c, s = _dft_cos_sin(FN_GROUP_DIM)
    eye = np.eye(width // FN_GROUP_DIM)
    return np.concatenate([np.kron(eye, c), np.kron(eye, s)], axis=0) * scale


def _fft_ctx_kernel(u_ref, fc_ref, cs_ref, o_ref):
    l = u_ref.shape[0]
    x = jnp.dot(fc_ref[...], u_ref[...], preferred_element_type=F32)
    xri = jnp.concatenate([x[:l], x[l:]], axis=1).astype(BF16)
    o_ref[...] = jnp.dot(xri, cs_ref[...], preferred_element_type=F32).astype(BF16)


def _fft_ctx(u, batch):
    n = u.shape[0]
    l = n // batch
    c, s = _dft_cos_sin(l)
    fc = _bf16_table(np.concatenate([c, -s], axis=0))
    cs = _bf16_table(_channel_dft(FN_WIDTH, (l * FN_GROUP_DIM) ** -0.5))
    spec = pl.BlockSpec((l, FN_WIDTH), lambda b: (b, 0))
    return pl.pallas_call(
        _fft_ctx_kernel,
        grid=(batch,),
        in_specs=[spec, pl.BlockSpec(fc.shape, lambda b: (0, 0)), pl.BlockSpec(cs.shape, lambda b: (0, 0))],
        out_specs=spec,
        out_shape=jax.ShapeDtypeStruct((n, FN_WIDTH), BF16),
        name="fourier_ctx",
    )(u, fc, cs)


def _fft_lat_kernel(u_ref, f1_ref, twr_ref, twi_ref, f2_ref, cs_ref, o_ref, z_ref, x2_ref):
    slabs = FFT_CB // LANES
    n1 = GRID_W
    x2_ref[...] = jnp.zeros(x2_ref.shape, F32)

    def stage1(n2, carry):
        u = jnp.concatenate([u_ref[j, pl.ds(n2, n1, stride=ROW_PITCH), :] for j in range(slabs)], axis=1)
        y = jnp.dot(f1_ref[...], u.astype(BF16), preferred_element_type=F32)
        yr, yi = y[:n1], y[n1:]
        tr = jnp.concatenate([twr_ref[n2]] * slabs, axis=1)
        ti = jnp.concatenate([twi_ref[n2]] * slabs, axis=1)
        zr = yr * tr - yi * ti
        zi = yr * ti + yi * tr
        for j in range(slabs):
            sl = slice(j * LANES, (j + 1) * LANES)
            z_ref[0, j, pl.ds(n2, n1, stride=ROW_PITCH), :] = zr[:, sl]
            z_ref[1, j, pl.ds(n2, n1, stride=ROW_PITCH), :] = zi[:, sl]
        return carry

    lax.fori_loop(0, GRID_W, stage1, 0, unroll=FFT_UNROLL)

    def stage2(k1, carry):
        base = pl.multiple_of(k1 * ROW_PITCH, 8)
        zr = jnp.concatenate([z_ref[0, j, pl.ds(base, GRID_W), :] for j in range(slabs)], axis=1)
        zi = jnp.concatenate([z_ref[1, j, pl.ds(base, GRID_W), :] for j in range(slabs)], axis=1)
        zz = jnp.concatenate([zr, zi], axis=0).astype(BF16)
        x = jnp.dot(f2_ref[...], zz, preferred_element_type=F32)
        for j in range(slabs):
            sl = slice(j * LANES, (j + 1) * LANES)
            x2_ref[j, pl.ds(k1, GRID_W, stride=ROW_PITCH), :] = x[:GRID_W, sl]
            x2_ref[slabs + j, pl.ds(k1, GRID_W, stride=ROW_PITCH), :] = x[GRID_W:, sl]
        return carry

    lax.fori_loop(0, n1, stage2, 0, unroll=FFT_UNROLL)

    chunk = 8 * ROW_PITCH
    for c in range(GRID_W * ROW_PITCH // chunk):
        rs = slice(c * chunk, (c + 1) * chunk)
        xri = jnp.concatenate([x2_ref[s, rs, :] for s in range(2 * slabs)], axis=1).astype(BF16)
        o_ref[rs, :] = jnp.dot(xri, cs_ref[...], preferred_element_type=F32)


def _fft_lat(u_pad, batch):
    l = GRID_W * GRID_W
    rows_pad = GRID_W * ROW_PITCH
    slabs = FFT_CB // LANES
    c, s = _dft_cos_sin(GRID_W)
    f1 = _bf16_table(np.concatenate([c, -s], axis=0))
    f2 = _bf16_table(np.block([[c, s], [-s, c]]))
    n2k1 = (np.arange(GRID_W)[:, None] * np.arange(GRID_W)[None, :]) % l
    ang = 2.0 * np.pi * n2k1 / l
    twr = jnp.broadcast_to(jnp.asarray(np.cos(ang), F32)[:, :, None], (GRID_W, GRID_W, LANES))
    twi = jnp.broadcast_to(jnp.asarray(-np.sin(ang), F32)[:, :, None], (GRID_W, GRID_W, LANES))
    cs = _bf16_table(_channel_dft(FFT_CB, (l * FN_GROUP_DIM) ** -0.5))
    const2 = lambda b, cb: (0, 0)
    const3 = lambda b, cb: (0, 0, 0)
    return pl.pallas_call(
        _fft_lat_kernel,
        grid=(batch, FN_WIDTH // FFT_CB),
        in_specs=[
            pl.BlockSpec((None, slabs, rows_pad, LANES), lambda b, cb: (b, cb, 0, 0)),
            pl.BlockSpec(f1.shape, const2),
            pl.BlockSpec(twr.shape, const3),
            pl.BlockSpec(twi.shape, const3),
            pl.BlockSpec(f2.shape, const2),
            pl.BlockSpec(cs.shape, const2),
        ],
        out_specs=pl.BlockSpec((None, rows_pad, FFT_CB), lambda b, cb: (b, 0, cb)),
        out_shape=jax.ShapeDtypeStruct((batch, rows_pad, FN_WIDTH), F32),
        scratch_shapes=[
            pltpu.VMEM((2, slabs, rows_pad, LANES), F32),
            pltpu.VMEM((2 * slabs, rows_pad, LANES), F32),
        ],
        compiler_params=pltpu.CompilerParams(vmem_limit_bytes=VMEM_LIMIT),
        name="fourier_lat",
    )(u_pad, f1, twr, twi, f2, cs)


def _outproj_kernel(x_ref, a_ref, f_ref, m_ref, w_ref, o_ref, *, is_ctx):
    if is_ctx:
        f = f_ref[...]
    else:
        f = jnp.concatenate(
            [f_ref[r * ROW_PITCH:r * ROW_PITCH + GRID_W, :] for r in range(TM // GRID_W)], axis=0).astype(BF16)
    af = jnp.concatenate([a_ref[...], f], axis=1)
    mix = jnp.dot(af, w_ref[...], preferred_element_type=F32)
    o_ref[...] = x_ref[...] + _mod_chunk(m_ref, 2) * mix


def _outproj(x2, a, f, mods4, row_of_tile, w_out, is_ctx, batch):
    n = x2.shape[0]
    tile = lambda i: (i, 0)
    if is_ctx:
        fspec = pl.BlockSpec((TM, FN_WIDTH), tile)
    else:
        tiles_per_batch = n // batch // TM
        fspec = pl.BlockSpec((None, TM // GRID_W * ROW_PITCH, FN_WIDTH),
                             lambda i: (i // tiles_per_batch, i % tiles_per_batch, 0))
    return pl.pallas_call(
        functools.partial(_outproj_kernel, is_ctx=is_ctx),
        grid=(n // TM,),
        in_specs=[
            pl.BlockSpec((TM, D_MODEL), tile),
            pl.BlockSpec((TM, NA_WIDTH), tile),
            fspec,
            _mods_spec(0, row_of_tile),
            pl.BlockSpec((D_MODEL, D_MODEL), lambda i: (0, 0)),
        ],
        out_specs=pl.BlockSpec((TM, D_MODEL), tile),
        out_shape=jax.ShapeDtypeStruct((n, D_MODEL), F32),
        name="outproj_ctx" if is_ctx else "outproj_lat",
    )(x2, a, f, mods4, w_out)


def _sg_kernel(x_ref, m_ref, g_ref, win_ref, vg_ref, ws_ref, bs_ref, wout_ref, o_ref):
    x = x_ref[...]
    h = _rms_mod(x, g_ref[...], _mod_chunk(m_ref, 0), _mod_chunk(m_ref, 1)).astype(BF16)
    z = jax.nn.gelu(jnp.dot(h, win_ref[...], preferred_element_type=F32))
    gated = []
    for g in range(SG_GROUPS):
        gs = slice(g * SG_GROUP_DIM, (g + 1) * SG_GROUP_DIM)
        u = z[:, gs]
        v = z[:, SG_WIDTH + g * SG_GROUP_DIM:SG_WIDTH + (g + 1) * SG_GROUP_DIM]
        ms = jnp.mean(v * v, axis=-1, keepdims=True)
        vn = (v * lax.rsqrt(ms + EPS) * vg_ref[:, gs]).astype(BF16)
        w = ws_ref[g]
        s = jnp.concatenate(
            [jnp.dot(w, vn[c * CHUNK:(c + 1) * CHUNK], preferred_element_type=F32) for c in range(TM // CHUNK)],
            axis=0)
        bias = jnp.concatenate([bs_ref[:, gs]] * (TM // CHUNK), axis=0)
        gated.append((u * (s + bias)).astype(BF16))
    mix = jnp.dot(jnp.concatenate(gated, axis=1), wout_ref[...], preferred_element_type=F32)
    o_ref[...] = x + _mod_chunk(m_ref, 2) * mix


def _spatial_gating(x2, mods4, row_of_tile, g, w_in, v_gain, w_s, b_full, w_out):
    n = x2.shape[0]
    tile = lambda i: (i, 0)
    const = lambda i: (0, 0)
    return pl.pallas_call(
        _sg_kernel,
        grid=(n // TM,),
        in_specs=[
            pl.BlockSpec((TM, D_MODEL), tile),
            _mods_spec(1, row_of_tile),
            pl.BlockSpec((1, D_MODEL), const),
            pl.BlockSpec((D_MODEL, 2 * SG_WIDTH), const),
            pl.BlockSpec((1, SG_WIDTH), const),
            pl.BlockSpec((SG_GROUPS, CHUNK, CHUNK), lambda i: (0, 0, 0)),
            pl.BlockSpec((CHUNK, SG_WIDTH), const),
            pl.BlockSpec((SG_WIDTH, D_MODEL), const),
        ],
        out_specs=pl.BlockSpec((TM, D_MODEL), tile),
        out_shape=jax.ShapeDtypeStruct((n, D_MODEL), F32),
        name="spatial_gating",
    )(x2, mods4, g, w_in, v_gain, w_s, b_full, w_out)


def _first_index_of(mask, idx, size):
    return jnp.min(jnp.where(mask, idx, size), axis=0, keepdims=True)


def _route(h, h_hi, rw_ref, rb_ref):
    h_lo = (h - h_hi.astype(F32)).astype(BF16)
    rw = rw_ref[...]
    rw_hi = rw.astype(BF16)
    rw_lo = (rw - rw_hi.astype(F32)).astype(BF16)
    logits = (jnp.dot(h_hi, rw_hi, preferred_element_type=F32)
              + jnp.dot(h_lo, rw_hi, preferred_element_type=F32)
              + jnp.dot(h_hi, rw_lo, preferred_element_type=F32))
    logits = logits.T[:N_EXPERTS]
    scores = jax.nn.sigmoid(logits)
    sel = scores + rb_ref[...]
    eg = EXPERTS_PER_GROUP
    ri = lax.broadcasted_iota(jnp.int32, (eg, TM), 0)
    firsts, seconds, group_scores = [], [], []
    for g in range(N_EXPERT_GROUPS):
        a = sel[g * eg:(g + 1) * eg]
        m1 = a.max(axis=0, keepdims=True)
        i1 = _first_index_of(a == m1, ri, eg)
        rest = jnp.where(ri == i1, -jnp.inf, a)
        m2 = rest.max(axis=0, keepdims=True)
        i2 = _first_index_of(rest == m2, ri, eg)
        firsts.append(i1)
        seconds.append(i2)
        group_scores.append(m1 + m2)
    gs = jnp.concatenate(group_scores, axis=0)
    gi = lax.broadcasted_iota(jnp.int32, gs.shape, 0)
    g_best = _first_index_of(gs == gs.max(axis=0, keepdims=True), gi, N_EXPERT_GROUPS)
    picked = []
    for g in range(N_EXPERT_GROUPS):
        chosen = (g_best == g) & ((ri == firsts[g]) | (ri == seconds[g]))
        picked.append(jnp.where(chosen, scores[g * eg:(g + 1) * eg], 0.0))
    w = jnp.concatenate(picked, axis=0)
    return w / w.sum(axis=0, keepdims=True), g_best


def _split_bf16(t):
    hi = t.astype(BF16)
    return hi, (t - hi.astype(F32)).astype(BF16)


def _tn_dot(a, b):
    return lax.dot_general(a, b, (((0,), (0,)), ((), ())), preferred_element_type=F32)


def _moe_kernel(x_ref, m_ref, g_ref, rw_ref, rb_ref, tri_ref, wg_ref, wu_ref, wd_ref, o_ref, y_ref):
    x = x_ref[...]
    h = _rms_mod(x, g_ref[...], _mod_chunk(m_ref, 3), _mod_chunk(m_ref, 4))
    h_hi = h.astype(BF16)
    w, g_best = _route(h, h_hi, rw_ref, rb_ref)
    wt = jnp.concatenate([w, jnp.zeros((LANES - N_EXPERTS, TM), F32)], axis=0)
    gates_hi, gates_lo = _split_bf16(wt.T)

    gi = lax.broadcasted_iota(jnp.int32, (2 * N_EXPERT_GROUPS, TM), 0)
    onehot = jnp.where(gi == g_best, 1.0, 0.0)
    prefix = jnp.dot(onehot.astype(BF16), tri_ref[...], preferred_element_type=F32)
    rank = jnp.sum(onehot * prefix, axis=0, keepdims=True)
    slot = lax.broadcasted_iota(jnp.int32, (MOE_CHUNK, TM), 0).astype(F32)

    y_ref[...] = jnp.zeros(y_ref.shape, F32)
    eg = EXPERTS_PER_GROUP
    for g in range(N_EXPERT_GROUPS):
        count = jnp.sum(onehot[g:g + 1]).astype(jnp.int32)

        def chunk(j, carry, g=g):
            key = jnp.where(g_best == g, rank - (j * MOE_CHUNK).astype(F32), -1.0)
            perm = jnp.where(slot == key, 1.0, 0.0).astype(BF16)
            hs = jnp.dot(perm, h_hi, preferred_element_type=F32).astype(BF16)
            gs = (jnp.dot(perm, gates_hi, preferred_element_type=F32)
                  + jnp.dot(perm, gates_lo, preferred_element_type=F32))
            y = jnp.zeros((MOE_CHUNK, D_MODEL), F32)
            for e in range(g * eg, (g + 1) * eg):
                a = jnp.dot(hs, wg_ref[e], preferred_element_type=F32)
                b = jnp.dot(hs, wu_ref[e], preferred_element_type=F32)
                hid = jax.nn.silu(a) * b * gs[:, e:e + 1]
                y = y + jnp.dot(hid.astype(BF16), wd_ref[e], preferred_element_type=F32)
            y_ref[...] += _tn_dot(perm, y.astype(BF16))
            return carry

        lax.fori_loop(0, (count + MOE_CHUNK - 1) // MOE_CHUNK, chunk, 0)
    o_ref[...] = x + _mod_chunk(m_ref, 5) * y_ref[...]


def _moe(x2, mods4, layer, row_of_tile, g, rw, rb, wg, wu, wd):
    n = x2.shape[0]
    tile = lambda i: (i, 0)
    const = lambda i: (0, 0)
    resident = lambda shape: pl.BlockSpec(shape, lambda i: (0, 0, 0), pipeline_mode=pl.Buffered(1))
    tri = jnp.asarray(np.triu(np.ones((TM, TM), np.float32), k=1), BF16)
    return pl.pallas_call(
        _moe_kernel,
        grid=(n // TM,),
        in_specs=[
            pl.BlockSpec((TM, D_MODEL), tile),
            _mods_spec(layer, row_of_tile),
            pl.BlockSpec((1, D_MODEL), const),
            pl.BlockSpec((D_MODEL, LANES), const),
            pl.BlockSpec((N_EXPERTS, 1), const),
            pl.BlockSpec((TM, TM), const),
            resident(wg.shape), resident(wu.shape), resident(wd.shape),
        ],
        out_specs=pl.BlockSpec((TM, D_MODEL), tile),
        out_shape=jax.ShapeDtypeStruct((n, D_MODEL), F32),
        scratch_shapes=[pltpu.VMEM((TM, D_MODEL), F32)],
        compiler_params=pltpu.CompilerParams(vmem_limit_bytes=VMEM_LIMIT),
        name="moe",
    )(x2, mods4, g, rw, rb, tri, wg, wu, wd)


def _trunk(x, mods4, row_of_tile, cache, p, is_ctx):
    batch, l, _ = x.shape
    x2 = x.reshape(batch * l, D_MODEL)

    outs = _inproj(x2, mods4, row_of_tile, p["norm1_g"][0], p["na_w_in"], p["q_gain"], p["k_gain"], p["head_bd"],
                   is_ctx, batch)
    if is_ctx:
        q, k, v, u, k32, v32 = outs
        a = _attn_ctx(q, k, v, batch)
        f = _fft_ctx(u, batch)
    else:
        q, k, v, u_pad = outs
        a = _attn_lat(q, k, v, cache[0], cache[1], p["bias_tbl"], batch)
        f = _fft_lat(u_pad, batch)
        k32 = v32 = None
    x2 = _outproj(x2, a, f, mods4, row_of_tile, p["na_w_out"], is_ctx, batch)
    x2 = _moe(x2, mods4, 0, row_of_tile, p["norm2_g"][0], p["router_w"], p["router_b"],
              p["moe_wg"][0], p["moe_wu"][0], p["moe_wd"][0])

    x2 = _spatial_gating(x2, mods4, row_of_tile, p["norm1_g"][1], p["sg_w_in"], p["sg_v_gain"], p["sg_w_s"],
                         p["sg_b_full"], p["sg_w_out"])
    x2 = _moe(x2, mods4, 1, row_of_tile, p["norm2_g"][1], p["router_w"], p["router_b"],
              p["moe_wg"][1], p["moe_wu"][1], p["moe_wd"][1])
    return x2.reshape(batch, l, D_MODEL), k32, v32


def kernel(x_prompt, x_sample, cache_k, cache_v, c, c_ctx, ada_w, ada_b, norm1_g, norm2_g, na_w_in, na_w_out,
           q_gain, k_gain, rpb, sg_w_in, sg_w_out, sg_v_gain, sg_w_s, sg_b_s, router_w, router_b,
           moe_w_gate, moe_w_up, moe_w_down):
    batch, seq, _ = x_prompt.shape
    dec_batch, dec_seq, _ = x_sample.shape
    assert dec_batch <= CTX_ROW and dec_seq == GRID_W * GRID_W and (batch * seq) % TM == 0
    assert ada_w.shape[0] == 2 and na_w_in.shape[0] == 1 and sg_w_in.shape[0] == 1

    cond = jnp.zeros((N_COND_ROWS, D_MODEL), F32).at[:dec_batch].set(c).at[CTX_ROW].set(c_ctx)
    mods = _ada_table(cond, ada_w, ada_b)
    mods4 = mods.reshape(mods.shape[0], N_COND_ROWS, 1, mods.shape[2])

    tile_rep = lambda t: jnp.tile(t, NA_HEADS).reshape(1, NA_WIDTH)
    p = {
        "norm1_g": norm1_g.reshape(-1, 1, D_MODEL),
        "norm2_g": norm2_g.reshape(-1, 1, D_MODEL),
        "na_w_in": na_w_in[0].astype(BF16),
        "na_w_out": na_w_out[0].astype(BF16),
        "q_gain": tile_rep(q_gain[0]),
        "k_gain": tile_rep(k_gain[0]),
        "head_bd": jnp.asarray(np.kron(np.eye(NA_HEADS), np.full((HEAD_DIM, HEAD_DIM), 1.0 / HEAD_DIM)), BF16),
        "bias_tbl": _bias_table(rpb[0], dec_seq // GRID_W),
        "sg_w_in": sg_w_in[0].astype(BF16),
        "sg_w_out": sg_w_out[0].astype(BF16),
        "sg_v_gain": sg_v_gain[0].reshape(1, SG_WIDTH),
        "sg_w_s": sg_w_s[0].astype(BF16),
        "sg_b_full": jnp.broadcast_to(sg_b_s[0].T[:, :, None], (CHUNK, SG_GROUPS, SG_GROUP_DIM)).reshape(
            CHUNK, SG_WIDTH),
        "router_w": jnp.pad(router_w, ((0, 0), (0, LANES - N_EXPERTS))),
        "router_b": router_b.reshape(N_EXPERTS, 1),
        "moe_wg": moe_w_gate.astype(BF16),
        "moe_wu": moe_w_up.astype(BF16),
        "moe_wd": moe_w_down.astype(BF16),
    }

    y_prompt, k32, v32 = _trunk(x_prompt, mods4, lambda i: CTX_ROW, None, p, True)
    k_ctx = k32.reshape(batch, 1, seq, NA_HEADS, HEAD_DIM)
    v_ctx = v32.reshape(batch, 1, seq, NA_HEADS, HEAD_DIM)

    tiles_per_batch = dec_seq // TM
    cache = (cache_k[:, 0].reshape(dec_batch, -1, NA_WIDTH).astype(BF16),
             cache_v[:, 0].reshape(dec_batch, -1, NA_WIDTH).astype(BF16))
    y_sample, _, _ = _trunk(x_sample, mods4, lambda i: i // tiles_per_batch, cache, p, False)
    return (y_prompt, y_sample, k_ctx, v_ctx)
```

```python
import functools

import numpy as np
import jax
import jax.numpy as jnp
from jax import lax
from jax.experimental import pallas as pl
from jax.experimental.pallas import tpu as pltpu

F32 = jnp.float32
BF16 = jnp.bfloat16

D_MODEL = 1024
GRID_W = 64
HEAD_DIM = 64
NA_WIDTH = 512
NA_HEADS = 8
WIN_ROWS = 8
WIN_COLS = 16
FN_WIDTH = 512
FN_GROUP_DIM = 64
SG_WIDTH = 1024
SG_GROUPS = 4
SG_GROUP_DIM = 256
CHUNK = 128
N_EXPERTS = 16
N_EXPERT_GROUPS = 4
EXPERTS_PER_GROUP = 4
EXPERT_FF = 256
EPS = 1e-6
NEG_INF = -1e30

LANES = 128
TM = 512
MOE_CHUNK = 160
ROW_GROUP = 4
GROUP_WIN_ROWS = 12
ROW_PITCH = 72
N_COND_ROWS = 8
CTX_ROW = 4
ADA_TN = 1536
FFT_UNROLL = 4
FFT_CB = 256
VMEM_LIMIT = 56 * 1024 * 1024


def _rms_mod(x, g, shift, scale):
    ms = jnp.mean(x * x, axis=-1, keepdims=True)
    y = x * lax.rsqrt(ms + EPS) * g
    return y * (1.0 + scale) + shift


def _mod_chunk(m_ref, i):
    return m_ref[:, i * D_MODEL:(i + 1) * D_MODEL]


def _ada_kernel(c_ref, w_ref, b_ref, o_ref):
    s = jax.nn.silu(c_ref[...]).astype(BF16)
    o_ref[...] = jnp.dot(s, w_ref[...].astype(BF16), preferred_element_type=F32) + b_ref[...]


def _ada_table(cond, ada_w, ada_b):
    depth = ada_w.shape[0]
    n_out = ada_w.shape[2]
    return pl.pallas_call(
        _ada_kernel,
        grid=(depth, n_out // ADA_TN),
        in_specs=[
            pl.BlockSpec((N_COND_ROWS, D_MODEL), lambda l, j: (0, 0)),
            pl.BlockSpec((None, D_MODEL, ADA_TN), lambda l, j: (l, 0, j)),
            pl.BlockSpec((None, 1, ADA_TN), lambda l, j: (l, 0, j)),
        ],
        out_specs=pl.BlockSpec((None, N_COND_ROWS, ADA_TN), lambda l, j: (l, 0, j)),
        out_shape=jax.ShapeDtypeStruct((depth, N_COND_ROWS, n_out), F32),
        name="ada_table",
    )(cond, ada_w, ada_b.reshape(depth, 1, n_out))


def _head_norm(t, bd_ref, gain):
    ms = jnp.dot((t * t).astype(BF16), bd_ref[...], preferred_element_type=F32)
    return t * lax.rsqrt(ms + EPS) * gain


def _inproj_kernel(x_ref, m_ref, g_ref, w_ref, qg_ref, kg_ref, bd_ref, *out_refs, is_ctx):
    h = _rms_mod(x_ref[...], g_ref[...], _mod_chunk(m_ref, 0), _mod_chunk(m_ref, 1)).astype(BF16)
    proj = jnp.dot(h, w_ref[...], preferred_element_type=F32)
    q = _head_norm(proj[:, 0:NA_WIDTH], bd_ref, qg_ref[...]) * (HEAD_DIM ** -0.5)
    k = _head_norm(proj[:, NA_WIDTH:2 * NA_WIDTH], bd_ref, kg_ref[...])
    v = proj[:, 2 * NA_WIDTH:3 * NA_WIDTH]
    u = proj[:, 3 * NA_WIDTH:]
    if is_ctx:
        q_ref, k_ref, v_ref, u_ref, k32_ref, v32_ref = out_refs
        k32_ref[...] = k
        v32_ref[...] = v
        u_ref[...] = u.astype(BF16)
    else:
        q_ref, k_ref, v_ref, u_ref = out_refs
        zeros = jnp.zeros((ROW_PITCH - GRID_W, LANES), F32)
        for r in range(TM // GRID_W):
            for j in range(FN_WIDTH // LANES):
                u_ref[j, r * ROW_PITCH:r * ROW_PITCH + GRID_W, :] = (
                    u[r * GRID_W:(r + 1) * GRID_W, j * LANES:(j + 1) * LANES])
                u_ref[j, r * ROW_PITCH + GRID_W:(r + 1) * ROW_PITCH, :] = zeros
    q_ref[...] = q.astype(BF16)
    k_ref[...] = k.astype(BF16)
    v_ref[...] = v.astype(BF16)


def _mods_spec(layer, row_of_tile):
    return pl.BlockSpec((None, None, 1, 6 * D_MODEL), lambda i: (layer, row_of_tile(i), 0, 0))


def _inproj(x2, mods4, row_of_tile, g, w_in, qg, kg, bd, is_ctx, batch):
    n = x2.shape[0]
    tile = lambda i: (i, 0)
    const = lambda i: (0, 0)
    bf = lambda: jax.ShapeDtypeStruct((n, NA_WIDTH), BF16)
    spec512 = pl.BlockSpec((TM, NA_WIDTH), tile)
    out_shape = [bf(), bf(), bf()]
    out_specs = [spec512, spec512, spec512]
    if is_ctx:
        out_shape += [bf(), jax.ShapeDtypeStruct((n, NA_WIDTH), F32), jax.ShapeDtypeStruct((n, NA_WIDTH), F32)]
        out_specs += [spec512, spec512, spec512]
    else:
        rows = n // batch // GRID_W
        tiles_per_batch = n // batch // TM
        pad_tile = TM // GRID_W * ROW_PITCH
        out_shape += [jax.ShapeDtypeStruct((batch, FN_WIDTH // LANES, rows * ROW_PITCH, LANES), F32)]
        out_specs += [pl.BlockSpec((None, FN_WIDTH // LANES, pad_tile, LANES),
                                   lambda i: (i // tiles_per_batch, 0, i % tiles_per_batch, 0))]
    return pl.pallas_call(
        functools.partial(_inproj_kernel, is_ctx=is_ctx),
        grid=(n // TM,),
        in_specs=[
            pl.BlockSpec((TM, D_MODEL), tile),
            _mods_spec(0, row_of_tile),
            pl.BlockSpec((1, D_MODEL), const),
            pl.BlockSpec((D_MODEL, 4 * NA_WIDTH), const),
            pl.BlockSpec((1, NA_WIDTH), const),
            pl.BlockSpec((1, NA_WIDTH), const),
            pl.BlockSpec((NA_WIDTH, NA_WIDTH), const),
        ],
        out_specs=out_specs,
        out_shape=out_shape,
        name="inproj_ctx" if is_ctx else "inproj_lat",
    )(x2, mods4, g, w_in, qg, kg, bd)


def _softmax_pv(parts):
    m = parts[0][0].max(axis=-1, keepdims=True)
    for s, _ in parts[1:]:
        m = jnp.maximum(m, s.max(axis=-1, keepdims=True))
    l = None
    o = None
    for s, v in parts:
        e = jnp.exp(s - m)
        ls = e.sum(axis=-1, keepdims=True)
        pv = jnp.dot(e.astype(BF16), v, preferred_element_type=F32)
        l = ls if l is None else l + ls
        o = pv if o is None else o + pv
    return o / l


def _nt_dot(a, b):
    return lax.dot_general(a, b, (((1,), (1,)), ((), ())), preferred_element_type=F32)


def _attn_ctx_kernel(q_ref, k_ref, v_ref, o_ref):
    lane = lax.broadcasted_iota(jnp.int32, (q_ref.shape[0], LANES), 1)
    even = lane < HEAD_DIM
    for p in range(NA_HEADS // 2):
        sl = slice(p * LANES, (p + 1) * LANES)
        qp, kp, vp = q_ref[:, sl], k_ref[:, sl], v_ref[:, sl]
        outs = []
        for half in range(2):
            qm = jnp.where(even if half == 0 else ~even, qp, jnp.zeros_like(qp))
            outs.append(_softmax_pv([(_nt_dot(qm, kp), vp)]))
        o_ref[:, sl] = jnp.where(even, outs[0], outs[1]).astype(BF16)


def _attn_ctx(q, k, v, batch):
    n = q.shape[0]
    l = n // batch
    spec = pl.BlockSpec((l, NA_WIDTH), lambda b: (b, 0))
    return pl.pallas_call(
        _attn_ctx_kernel,
        grid=(batch,),
        in_specs=[spec, spec, spec],
        out_specs=spec,
        out_shape=jax.ShapeDtypeStruct((n, NA_WIDTH), BF16),
        name="attn_ctx",
    )(q, k, v)


def _attn_lat_kernel(q_ref, k_ref, v_ref, kc_ref, vc_ref, bias_ref, o_ref, *, rows):
    g = pl.program_id(1)
    base = jnp.clip(ROW_GROUP * g - WIN_ROWS // 2, 0, rows - GROUP_WIN_ROWS)
    start = pl.multiple_of(base * GRID_W, GRID_W)
    win = pl.ds(start, GROUP_WIN_ROWS * GRID_W)
    lane = lax.broadcasted_iota(jnp.int32, (ROW_GROUP * GRID_W, LANES), 1)
    even = lane < HEAD_DIM
    for p in range(NA_HEADS // 2):
        sl = slice(p * LANES, (p + 1) * LANES)
        qp = q_ref[:, sl]
        kw, vw = k_ref[win, sl], v_ref[win, sl]
        kc, vc = kc_ref[:, sl], vc_ref[:, sl]
        outs = []
        for half in range(2):
            qm = jnp.where(even if half == 0 else ~even, qp, jnp.zeros_like(qp))
            s_lat = _nt_dot(qm, kw) + bias_ref[2 * p + half]
            s_ctx = _nt_dot(qm, kc)
            outs.append(_softmax_pv([(s_lat, vw), (s_ctx, vc)]))
        o_ref[:, sl] = jnp.where(even, outs[0], outs[1]).astype(BF16)


def _attn_lat(q, k, v, kc, vc, bias_tbl, batch):
    n = q.shape[0]
    l = n // batch
    rows = l // GRID_W
    groups = rows // ROW_GROUP
    q3, k3, v3 = (t.reshape(batch, l, NA_WIDTH) for t in (q, k, v))
    full = pl.BlockSpec((None, l, NA_WIDTH), lambda b, g: (b, 0, 0))
    ctx = pl.BlockSpec((None, kc.shape[1], NA_WIDTH), lambda b, g: (b, 0, 0))
    qspec = pl.BlockSpec((None, ROW_GROUP * GRID_W, NA_WIDTH), lambda b, g: (b, g, 0))
    bias_class = lambda b, g: (jnp.where(g == 0, 0, jnp.where(g == groups - 1, 2, 1)), 0, 0, 0)
    out = pl.pallas_call(
        functools.partial(_attn_lat_kernel, rows=rows),
        grid=(batch, groups),
        in_specs=[
            qspec, full, full, ctx, ctx,
            pl.BlockSpec((None, NA_HEADS, ROW_GROUP * GRID_W, GROUP_WIN_ROWS * GRID_W), bias_class),
        ],
        out_specs=qspec,
        out_shape=jax.ShapeDtypeStruct((batch, l, NA_WIDTH), BF16),
        compiler_params=pltpu.CompilerParams(vmem_limit_bytes=VMEM_LIMIT),
        name="attn_lat",
    )(q3, k3, v3, kc, vc, bias_tbl)
    return out.reshape(n, NA_WIDTH)


def _bias_table(rpb_i, rows):
    groups = rows // ROW_GROUP
    assert rows % ROW_GROUP == 0 and rows >= GROUP_WIN_ROWS and groups >= 3
    col = np.arange(GRID_W)
    dc = np.clip(col[None, :] - col[:, None] + (WIN_COLS - 1), 0, 2 * WIN_COLS - 2)
    col_start = np.clip(col - WIN_COLS // 2, 0, GRID_W - WIN_COLS)
    col_valid = (col[None, :] >= col_start[:, None]) & (col[None, :] < col_start[:, None] + WIN_COLS)
    col_onehot = (dc[:, :, None] == np.arange(2 * WIN_COLS - 1)).astype(np.float32)
    row_onehot = np.zeros((3, ROW_GROUP, GROUP_WIN_ROWS, 2 * WIN_ROWS - 1), np.float32)
    row_valid = np.zeros((3, ROW_GROUP, GROUP_WIN_ROWS), bool)
    for cls, g in enumerate((0, 1, groups - 1)):
        base = np.clip(ROW_GROUP * g - WIN_ROWS // 2, 0, rows - GROUP_WIN_ROWS)
        for j in range(ROW_GROUP):
            r = ROW_GROUP * g + j
            row_start = np.clip(r - WIN_ROWS // 2, 0, rows - WIN_ROWS)
            for u in range(GROUP_WIN_ROWS):
                if row_start <= base + u < row_start + WIN_ROWS:
                    row_onehot[cls, j, u, base + u - r + WIN_ROWS - 1] = 1.0
                    row_valid[cls, j, u] = True
    t = jnp.einsum("hrc,xjur,qkc->xhjquk", rpb_i.astype(F32), row_onehot, col_onehot,
                   precision=lax.Precision.HIGHEST)
    valid = row_valid[:, None, :, None, :, None] & col_valid[None, None, None, :, None, :]
    t = jnp.where(valid, t, NEG_INF)
    return t.reshape(3, NA_HEADS, ROW_GROUP * GRID_W, GROUP_WIN_ROWS * GRID_W)


def _bf16_table(a):
    return jnp.asarray(a, F32).astype(BF16)


def _dft_cos_sin(n):
    idx = np.arange(n)
    ang = 2.0 * np.pi * ((idx[:, None] * idx[None, :]) % n) / n
    return np.cos(ang), np.sin(ang)


def _channel_dft(width, scale):
    c, s = _dft_cos_sin(FN_GROUP_DIM)
    eye = np.eye(width // FN_GROUP_DIM)
    return np.concatenate([np.kron(eye, c), np.kron(eye, s)], axis=0) * scale


def _fft_ctx_kernel(u_ref, fc_ref, cs_ref, o_ref):
    l = u_ref.shape[0]
    x = jnp.dot(fc_ref[...], u_ref[...], preferred_element_type=F32)
    xri = jnp.concatenate([x[:l], x[l:]], axis=1).astype(BF16)
    o_ref[...] = jnp.dot(xri, cs_ref[...], preferred_element_type=F32).astype(BF16)


def _fft_ctx(u, batch):
    n = u.shape[0]
    l = n // batch
    c, s = _dft_cos_sin(l)
    fc = _bf16_table(np.concatenate([c, -s], axis=0))
    cs = _bf16_table(_channel_dft(FN_WIDTH, (l * FN_GROUP_DIM) ** -0.5))
    spec = pl.BlockSpec((l, FN_WIDTH), lambda b: (b, 0))
    return pl.pallas_call(
        _fft_ctx_kernel,
        grid=(batch,),
        in_specs=[spec, pl.BlockSpec(fc.shape, lambda b: (0, 0)), pl.BlockSpec(cs.shape, lambda b: (0, 0))],
        out_specs=spec,
        out_shape=jax.ShapeDtypeStruct((n, FN_WIDTH), BF16),
        name="fourier_ctx",
    )(u, fc, cs)


def _fft_lat_kernel(u_ref, f1_ref, twr_ref, twi_ref, f2_ref, cs_ref, o_ref, z_ref, x2_ref):
    slabs = FFT_CB // LANES
    n1 = GRID_W
    x2_ref[...] = jnp.zeros(x2_ref.shape, F32)

    def stage1(n2, carry):
        u = jnp.concatenate([u_ref[j, pl.ds(n2, n1, stride=ROW_PITCH), :] for j in range(slabs)], axis=1)
        y = jnp.dot(f1_ref[...], u.astype(BF16), preferred_element_type=F32)
        yr, yi = y[:n1], y[n1:]
        tr = jnp.concatenate([twr_ref[n2]] * slabs, axis=1)
        ti = jnp.concatenate([twi_ref[n2]] * slabs, axis=1)
        zr = yr * tr - yi * ti
        zi = yr * ti + yi * tr
        for j in range(slabs):
            sl = slice(j * LANES, (j + 1) * LANES)
            z_ref[0, j, pl.ds(n2, n1, stride=ROW_PITCH), :] = zr[:, sl]
            z_ref[1, j, pl.ds(n2, n1, stride=ROW_PITCH), :] = zi[:, sl]
        return carry

    lax.fori_loop(0, GRID_W, stage1, 0, unroll=FFT_UNROLL)

    def stage2(k1, carry):
        base = pl.multiple_of(k1 * ROW_PITCH, 8)
        zr = jnp.concatenate([z_ref[0, j, pl.ds(base, GRID_W), :] for j in range(slabs)], axis=1)
        zi = jnp.concatenate([z_ref[1, j, pl.ds(base, GRID_W), :] for j in range(slabs)], axis=1)
        zz = jnp.concatenate([zr, zi], axis=0).astype(BF16)
        x = jnp.dot(f2_ref[...], zz, preferred_element_type=F32)
        for j in range(slabs):
            sl = slice(j * LANES, (j + 1) * LANES)
            x2_ref[j, pl.ds(k1, GRID_W, stride=ROW_PITCH), :] = x[:GRID_W, sl]
            x2_ref[slabs + j, pl.ds(k1, GRID_W, stride=ROW_PITCH), :] = x[GRID_W:, sl]
        return carry

    lax.fori_loop(0, n1, stage2, 0, unroll=FFT_UNROLL)

    chunk = 8 * ROW_PITCH
    for c in range(GRID_W * ROW_PITCH // chunk):
        rs = slice(c * chunk, (c + 1) * chunk)
        xri = jnp.concatenate([x2_ref[s, rs, :] for s in range(2 * slabs)], axis=1).astype(BF16)
        o_ref[rs, :] = jnp.dot(xri, cs_ref[...], preferred_element_type=F32)


def _fft_lat(u_pad, batch):
    l = GRID_W * GRID_W
    rows_pad = GRID_W * ROW_PITCH
    slabs = FFT_CB // LANES
    c, s = _dft_cos_sin(GRID_W)
    f1 = _bf16_table(np.concatenate([c, -s], axis=0))
    f2 = _bf16_table(np.block([[c, s], [-s, c]]))
    n2k1 = (np.arange(GRID_W)[:, None] * np.arange(GRID_W)[None, :]) % l
    ang = 2.0 * np.pi * n2k1 / l
    twr = jnp.broadcast_to(jnp.asarray(np.cos(ang), F32)[:, :, None], (GRID_W, GRID_W, LANES))
    twi = jnp.broadcast_to(jnp.asarray(-np.sin(ang), F32)[:, :, None], (GRID_W, GRID_W, LANES))
    cs = _bf16_table(_channel_dft(FFT_CB, (l * FN_GROUP_DIM) ** -0.5))
    const2 = lambda b, cb: (0, 0)
    const3 = lambda b, cb: (0, 0, 0)
    return pl.pallas_call(
        _fft_lat_kernel,
        grid=(batch, FN_WIDTH // FFT_CB),
        in_specs=[
            pl.BlockSpec((None, slabs, rows_pad, LANES), lambda b, cb: (b, cb, 0, 0)),
            pl.BlockSpec(f1.shape, const2),
            pl.BlockSpec(twr.shape, const3),
            pl.BlockSpec(twi.shape, const3),
            pl.BlockSpec(f2.shape, const2),
            pl.BlockSpec(cs.shape, const2),
        ],
        out_specs=pl.BlockSpec((None, rows_pad, FFT_CB), lambda b, cb: (b, 0, cb)),
        out_shape=jax.ShapeDtypeStruct((batch, rows_pad, FN_WIDTH), F32),
        scratch_shapes=[
            pltpu.VMEM((2, slabs, rows_pad, LANES), F32),
            pltpu.VMEM((2 * slabs, rows_pad, LANES), F32),
        ],
        compiler_params=pltpu.CompilerParams(vmem_limit_bytes=VMEM_LIMIT),
        name="fourier_lat",
    )(u_pad, f1, twr, twi, f2, cs)


def _outproj_kernel(x_ref, a_ref, f_ref, m_ref, w_ref, o_ref, *, is_ctx):
    if is_ctx:
        f = f_ref[...]
    else:
        f = jnp.concatenate(
            [f_ref[r * ROW_PITCH:r * ROW_PITCH + GRID_W, :] for r in range(TM // GRID_W)], axis=0).astype(BF16)
    af = jnp.concatenate([a_ref[...], f], axis=1)
    mix = jnp.dot(af, w_ref[...], preferred_element_type=F32)
    o_ref[...] = x_ref[...] + _mod_chunk(m_ref, 2) * mix


def _outproj(x2, a, f, mods4, row_of_tile, w_out, is_ctx, batch):
    n = x2.shape[0]
    tile = lambda i: (i, 0)
    if is_ctx:
        fspec = pl.BlockSpec((TM, FN_WIDTH), tile)
    else:
        tiles_per_batch = n // batch // TM
        fspec = pl.BlockSpec((None, TM // GRID_W * ROW_PITCH, FN_WIDTH),
                             lambda i: (i // tiles_per_batch, i % tiles_per_batch, 0))
    return pl.pallas_call(
        functools.partial(_outproj_kernel, is_ctx=is_ctx),
        grid=(n // TM,),
        in_specs=[
            pl.BlockSpec((TM, D_MODEL), tile),
            pl.BlockSpec((TM, NA_WIDTH), tile),
            fspec,
            _mods_spec(0, row_of_tile),
            pl.BlockSpec((D_MODEL, D_MODEL), lambda i: (0, 0)),
        ],
        out_specs=pl.BlockSpec((TM, D_MODEL), tile),
        out_shape=jax.ShapeDtypeStruct((n, D_MODEL), F32),
        name="outproj_ctx" if is_ctx else "outproj_lat",
    )(x2, a, f, mods4, w_out)


def _sg_kernel(x_ref, m_ref, g_ref, win_ref, vg_ref, ws_ref, bs_ref, wout_ref, o_ref):
    x = x_ref[...]
    h = _rms_mod(x, g_ref[...], _mod_chunk(m_ref, 0), _mod_chunk(m_ref, 1)).astype(BF16)
    z = jax.nn.gelu(jnp.dot(h, win_ref[...], preferred_element_type=F32))
    gated = []
    for g in range(SG_GROUPS):
        gs = slice(g * SG_GROUP_DIM, (g + 1) * SG_GROUP_DIM)
        u = z[:, gs]
        v = z[:, SG_WIDTH + g * SG_GROUP_DIM:SG_WIDTH + (g + 1) * SG_GROUP_DIM]
        ms = jnp.mean(v * v, axis=-1, keepdims=True)
        vn = (v * lax.rsqrt(ms + EPS) * vg_ref[:, gs]).astype(BF16)
        w = ws_ref[g]
        s = jnp.concatenate(
            [jnp.dot(w, vn[c * CHUNK:(c + 1) * CHUNK], preferred_element_type=F32) for c in range(TM // CHUNK)],
            axis=0)
        bias = jnp.concatenate([bs_ref[:, gs]] * (TM // CHUNK), axis=0)
        gated.append((u * (s + bias)).astype(BF16))
    mix = jnp.dot(jnp.concatenate(gated, axis=1), wout_ref[...], preferred_element_type=F32)
    o_ref[...] = x + _mod_chunk(m_ref, 2) * mix


def _spatial_gating(x2, mods4, row_of_tile, g, w_in, v_gain, w_s, b_full, w_out):
    n = x2.shape[0]
    tile = lambda i: (i, 0)
    const = lambda i: (0, 0)
    return pl.pallas_call(
        _sg_kernel,
        grid=(n // TM,),
        in_specs=[
            pl.BlockSpec((TM, D_MODEL), tile),
            _mods_spec(1, row_of_tile),
            pl.BlockSpec((1, D_MODEL), const),
            pl.BlockSpec((D_MODEL, 2 * SG_WIDTH), const),
            pl.BlockSpec((1, SG_WIDTH), const),
            pl.BlockSpec((SG_GROUPS, CHUNK, CHUNK), lambda i: (0, 0, 0)),
            pl.BlockSpec((CHUNK, SG_WIDTH), const),
            pl.BlockSpec((SG_WIDTH, D_MODEL), const),
        ],
        out_specs=pl.BlockSpec((TM, D_MODEL), tile),
        out_shape=jax.ShapeDtypeStruct((n, D_MODEL), F32),
        name="spatial_gating",
    )(x2, mods4, g, w_in, v_gain, w_s, b_full, w_out)


def _first_index_of(mask, idx, size):
    return jnp.min(jnp.where(mask, idx, size), axis=0, keepdims=True)


def _route(h, h_hi, rw_ref, rb_ref):
    h_lo = (h - h_hi.astype(F32)).astype(BF16)
    rw = rw_ref[...]
    rw_hi = rw.astype(BF16)
    rw_lo = (rw - rw_hi.astype(F32)).astype(BF16)
    hi_both = jnp.dot(h_hi, jnp.concatenate([rw_hi, rw_lo], axis=1), preferred_element_type=F32)
    logits = (hi_both[:, :LANES] + jnp.dot(h_lo, rw_hi, preferred_element_type=F32)
              + hi_both[:, LANES:])
    logits = logits.T[:N_EXPERTS]
    scores = jax.nn.sigmoid(logits)
    sel = scores + rb_ref[...]
    eg = EXPERTS_PER_GROUP
    ri = lax.broadcasted_iota(jnp.int32, (eg, TM), 0)
    firsts, seconds, group_scores = [], [], []
    for g in range(N_EXPERT_GROUPS):
        a = sel[g * eg:(g + 1) * eg]
        m1 = a.max(axis=0, keepdims=True)
        i1 = _first_index_of(a == m1, ri, eg)
        rest = jnp.where(ri == i1, -jnp.inf, a)
        m2 = rest.max(axis=0, keepdims=True)
        i2 = _first_index_of(rest == m2, ri, eg)
        firsts.append(i1)
        seconds.append(i2)
        group_scores.append(m1 + m2)
    gs = jnp.concatenate(group_scores, axis=0)
    gi = lax.broadcasted_iota(jnp.int32, gs.shape, 0)
    g_best = _first_index_of(gs == gs.max(axis=0, keepdims=True), gi, N_EXPERT_GROUPS)
    picked = []
    for g in range(N_EXPERT_GROUPS):
        chosen = (g_best == g) & ((ri == firsts[g]) | (ri == seconds[g]))
        picked.append(jnp.where(chosen, scores[g * eg:(g + 1) * eg], 0.0))
    w = jnp.concatenate(picked, axis=0)
    return w / w.sum(axis=0, keepdims=True), g_best


def _split_bf16(t):
    hi = t.astype(BF16)
    return hi, (t - hi.astype(F32)).astype(BF16)


def _tn_dot(a, b):
    return lax.dot_general(a, b, (((0,), (0,)), ((), ())), preferred_element_type=F32)


def _moe_kernel(x_ref, m_ref, g_ref, rw_ref, rb_ref, tri_ref, wg_ref, wu_ref, wd_ref, o_ref):
    x = x_ref[...]
    h = _rms_mod(x, g_ref[...], _mod_chunk(m_ref, 3), _mod_chunk(m_ref, 4))
    h_hi = h.astype(BF16)
    w, g_best = _route(h, h_hi, rw_ref, rb_ref)
    wt = jnp.concatenate([w, jnp.zeros((LANES - N_EXPERTS, TM), F32)], axis=0)
    gates_hi, gates_lo = _split_bf16(wt.T)

    gi = lax.broadcasted_iota(jnp.int32, (2 * N_EXPERT_GROUPS, TM), 0)
    onehot = jnp.where(gi == g_best, 1.0, 0.0)
    prefix = jnp.dot(onehot.astype(BF16), tri_ref[...], preferred_element_type=F32)
    rank = jnp.sum(onehot * prefix, axis=0, keepdims=True)
    slot = lax.broadcasted_iota(jnp.int32, (MOE_CHUNK, TM), 0).astype(F32)

    o_ref[...] = x
    gate2 = _mod_chunk(m_ref, 5)
    eg = EXPERTS_PER_GROUP
    for g in range(N_EXPERT_GROUPS):
        count = jnp.sum(onehot[g:g + 1]).astype(jnp.int32)

        def chunk(j, carry, g=g):
            key = jnp.where(g_best == g, rank - (j * MOE_CHUNK).astype(F32), -1.0)
            perm = jnp.where(slot == key, 1.0, 0.0).astype(BF16)
            hs = jnp.dot(perm, h_hi, preferred_element_type=F32).astype(BF16)
            gs = (jnp.dot(perm, gates_hi, preferred_element_type=F32)
                  + jnp.dot(perm, gates_lo, preferred_element_type=F32))
            y = jnp.zeros((MOE_CHUNK, D_MODEL), F32)
            for e in range(g * eg, (g + 1) * eg):
                a = jnp.dot(hs, wg_ref[e], preferred_element_type=F32)
                b = jnp.dot(hs, wu_ref[e], preferred_element_type=F32)
                hid = jax.nn.silu(a) * b * gs[:, e:e + 1]
                y = y + jnp.dot(hid.astype(BF16), wd_ref[e], preferred_element_type=F32)
            o_ref[...] += _tn_dot(perm, (gate2 * y).astype(BF16))
            return carry

        lax.fori_loop(0, (count + MOE_CHUNK - 1) // MOE_CHUNK, chunk, 0)


def _moe(x2, mods4, layer, row_of_tile, g, rw, rb, wg, wu, wd):
    n = x2.shape[0]
    tile = lambda i: (i, 0)
    const = lambda i: (0, 0)
    resident = lambda shape: pl.BlockSpec(shape, lambda i: (0, 0, 0), pipeline_mode=pl.Buffered(1))
    tri = jnp.asarray(np.triu(np.ones((TM, TM), np.float32), k=1), BF16)
    return pl.pallas_call(
        _moe_kernel,
        grid=(n // TM,),
        in_specs=[
            pl.BlockSpec((TM, D_MODEL), tile),
            _mods_spec(layer, row_of_tile),
            pl.BlockSpec((1, D_MODEL), const),
            pl.BlockSpec((D_MODEL, LANES), const),
            pl.BlockSpec((N_EXPERTS, 1), const),
            pl.BlockSpec((TM, TM), const),
            resident(wg.shape), resident(wu.shape), resident(wd.shape),
        ],
        out_specs=pl.BlockSpec((TM, D_MODEL), tile),
        out_shape=jax.ShapeDtypeStruct((n, D_MODEL), F32),
        compiler_params=pltpu.CompilerParams(vmem_limit_bytes=VMEM_LIMIT),
        name="moe",
    )(x2, mods4, g, rw, rb, tri, wg, wu, wd)


def _trunk(x, mods4, row_of_tile, cache, p, is_ctx):
    batch, l, _ = x.shape
    x2 = x.reshape(batch * l, D_MODEL)

    outs = _inproj(x2, mods4, row_of_tile, p["norm1_g"][0], p["na_w_in"], p["q_gain"], p["k_gain"], p["head_bd"],
                   is_ctx, batch)
    if is_ctx:
        q, k, v, u, k32, v32 = outs
        a = _attn_ctx(q, k, v, batch)
        f = _fft_ctx(u, batch)
    else:
        q, k, v, u_pad = outs
        a = _attn_lat(q, k, v, cache[0], cache[1], p["bias_tbl"], batch)
        f = _fft_lat(u_pad, batch)
        k32 = v32 = None
    x2 = _outproj(x2, a, f, mods4, row_of_tile, p["na_w_out"], is_ctx, batch)
    x2 = _moe(x2, mods4, 0, row_of_tile, p["norm2_g"][0], p["router_w"], p["router_b"],
              p["moe_wg"][0], p["moe_wu"][0], p["moe_wd"][0])

    x2 = _spatial_gating(x2, mods4, row_of_tile, p["norm1_g"][1], p["sg_w_in"], p["sg_v_gain"], p["sg_w_s"],
                         p["sg_b_full"], p["sg_w_out"])
    x2 = _moe(x2, mods4, 1, row_of_tile, p["norm2_g"][1], p["router_w"], p["router_b"],
              p["moe_wg"][1], p["moe_wu"][1], p["moe_wd"][1])
    return x2.reshape(batch, l, D_MODEL), k32, v32


def kernel(x_prompt, x_sample, cache_k, cache_v, c, c_ctx, ada_w, ada_b, norm1_g, norm2_g, na_w_in, na_w_out,
           q_gain, k_gain, rpb, sg_w_in, sg_w_out, sg_v_gain, sg_w_s, sg_b_s, router_w, router_b,
           moe_w_gate, moe_w_up, moe_w_down):
    batch, seq, _ = x_prompt.shape
    dec_batch, dec_seq, _ = x_sample.shape
    assert dec_batch <= CTX_ROW and dec_seq == GRID_W * GRID_W and (batch * seq) % TM == 0
    assert ada_w.shape[0] == 2 and na_w_in.shape[0] == 1 and sg_w_in.shape[0] == 1

    cond = jnp.zeros((N_COND_ROWS, D_MODEL), F32).at[:dec_batch].set(c).at[CTX_ROW].set(c_ctx)
    mods = _ada_table(cond, ada_w, ada_b)
    mods4 = mods.reshape(mods.shape[0], N_COND_ROWS, 1, mods.shape[2])

    tile_rep = lambda t: jnp.tile(t, NA_HEADS).reshape(1, NA_WIDTH)
    p = {
        "norm1_g": norm1_g.reshape(-1, 1, D_MODEL),
        "norm2_g": norm2_g.reshape(-1, 1, D_MODEL),
        "na_w_in": na_w_in[0].astype(BF16),
        "na_w_out": na_w_out[0].astype(BF16),
        "q_gain": tile_rep(q_gain[0]),
        "k_gain": tile_rep(k_gain[0]),
        "head_bd": jnp.asarray(np.kron(np.eye(NA_HEADS), np.full((HEAD_DIM, HEAD_DIM), 1.0 / HEAD_DIM)), BF16),
        "bias_tbl": _bias_table(rpb[0], dec_seq // GRID_W),
        "sg_w_in": sg_w_in[0].astype(BF16),
        "sg_w_out": sg_w_out[0].astype(BF16),
        "sg_v_gain": sg_v_gain[0].reshape(1, SG_WIDTH),
        "sg_w_s": sg_w_s[0].astype(BF16),
        "sg_b_full": jnp.broadcast_to(sg_b_s[0].T[:, :, None], (CHUNK, SG_GROUPS, SG_GROUP_DIM)).reshape(
            CHUNK, SG_WIDTH),
        "router_w": jnp.pad(router_w, ((0, 0), (0, LANES - N_EXPERTS))),
        "router_b": router_b.reshape(N_EXPERTS, 1),
        "moe_wg": moe_w_gate.astype(BF16),
        "moe_wu": moe_w_up.astype(BF16),
        "moe_wd": moe_w_down.astype(BF16),
    }

    y_prompt, k32, v32 = _trunk(x_prompt, mods4, lambda i: CTX_ROW, None, p, True)
    k_ctx = k32.reshape(batch, 1, seq, NA_HEADS, HEAD_DIM)
    v_ctx = v32.reshape(batch, 1, seq, NA_HEADS, HEAD_DIM)

    tiles_per_batch = dec_seq // TM
    cache = (cache_k[:, 0].reshape(dec_batch, -1, NA_WIDTH).astype(BF16),
             cache_v[:, 0].reshape(dec_batch, -1, NA_WIDTH).astype(BF16))
    y_sample, _, _ = _trunk(x_sample, mods4, lambda i: i // tiles_per_batch, cache, p, False)
    return (y_prompt, y_sample, k_ctx, v_ctx)
```

```python
import functools

import numpy as np
import jax
import jax.numpy as jnp
from jax import lax
from jax.experimental import pallas as pl
from jax.experimental.pallas import tpu as pltpu

F32 = jnp.float32
BF16 = jnp.bfloat16

D_MODEL = 1024
GRID_W = 64
HEAD_DIM = 64
NA_WIDTH = 512
NA_HEADS = 8
WIN_ROWS = 8
WIN_COLS = 16
FN_WIDTH = 512
FN_GROUP_DIM = 64
SG_WIDTH = 1024
SG_GROUPS = 4
SG_GROUP_DIM = 256
CHUNK = 128
N_EXPERTS = 16
N_EXPERT_GROUPS = 4
EXPERTS_PER_GROUP = 4
EXPERT_FF = 256
EPS = 1e-6
NEG_INF = -1e30

LANES = 128
TM = 512
MOE_CHUNK = 160
ROW_GROUP = 4
GROUP_WIN_ROWS = 12
ROW_PITCH = 72
N_COND_ROWS = 8
CTX_ROW = 4
ADA_TN = 1536
FFT_UNROLL = 8
FFT_CB = 256
VMEM_LIMIT = 56 * 1024 * 1024


def _rms_mod(x, g, shift, scale):
    ms = jnp.mean(x * x, axis=-1, keepdims=True)
    y = x * lax.rsqrt(ms + EPS) * g
    return y * (1.0 + scale) + shift


def _mod_chunk(m_ref, i):
    return m_ref[:, i * D_MODEL:(i + 1) * D_MODEL]


def _ada_kernel(c_ref, w_ref, b_ref, o_ref):
    s = jax.nn.silu(c_ref[...]).astype(BF16)
    o_ref[...] = jnp.dot(s, w_ref[...].astype(BF16), preferred_element_type=F32) + b_ref[...]


def _ada_table(cond, ada_w, ada_b):
    depth = ada_w.shape[0]
    n_out = ada_w.shape[2]
    return pl.pallas_call(
        _ada_kernel,
        grid=(depth, n_out // ADA_TN),
        in_specs=[
            pl.BlockSpec((N_COND_ROWS, D_MODEL), lambda l, j: (0, 0)),
            pl.BlockSpec((None, D_MODEL, ADA_TN), lambda l, j: (l, 0, j)),
            pl.BlockSpec((None, 1, ADA_TN), lambda l, j: (l, 0, j)),
        ],
        out_specs=pl.BlockSpec((None, N_COND_ROWS, ADA_TN), lambda l, j: (l, 0, j)),
        out_shape=jax.ShapeDtypeStruct((depth, N_COND_ROWS, n_out), F32),
        name="ada_table",
    )(cond, ada_w, ada_b.reshape(depth, 1, n_out))


def _head_norm(t, bd_ref, gain):
    ms = jnp.dot((t * t).astype(BF16), bd_ref[...], preferred_element_type=F32)
    return t * lax.rsqrt(ms + EPS) * gain


def _inproj_kernel(x_ref, m_ref, g_ref, w_ref, qg_ref, kg_ref, bd_ref, *out_refs, is_ctx):
    h = _rms_mod(x_ref[...], g_ref[...], _mod_chunk(m_ref, 0), _mod_chunk(m_ref, 1)).astype(BF16)
    proj = jnp.dot(h, w_ref[...], preferred_element_type=F32)
    q = _head_norm(proj[:, 0:NA_WIDTH], bd_ref, qg_ref[...]) * (HEAD_DIM ** -0.5)
    k = _head_norm(proj[:, NA_WIDTH:2 * NA_WIDTH], bd_ref, kg_ref[...])
    v = proj[:, 2 * NA_WIDTH:3 * NA_WIDTH]
    u = proj[:, 3 * NA_WIDTH:]
    if is_ctx:
        q_ref, k_ref, v_ref, u_ref, k32_ref, v32_ref = out_refs
        k32_ref[...] = k
        v32_ref[...] = v
        u_ref[...] = u.astype(BF16)
    else:
        q_ref, k_ref, v_ref, u_ref = out_refs
        zeros = jnp.zeros((ROW_PITCH - GRID_W, LANES), F32)
        for r in range(TM // GRID_W):
            for j in range(FN_WIDTH // LANES):
                u_ref[j, r * ROW_PITCH:r * ROW_PITCH + GRID_W, :] = (
                    u[r * GRID_W:(r + 1) * GRID_W, j * LANES:(j + 1) * LANES])
                u_ref[j, r * ROW_PITCH + GRID_W:(r + 1) * ROW_PITCH, :] = zeros
    q_ref[...] = q.astype(BF16)
    k_ref[...] = k.astype(BF16)
    v_ref[...] = v.astype(BF16)


def _mods_spec(layer, row_of_tile):
    return pl.BlockSpec((None, None, 1, 6 * D_MODEL), lambda i: (layer, row_of_tile(i), 0, 0))


def _inproj(x2, mods4, row_of_tile, g, w_in, qg, kg, bd, is_ctx, batch):
    n = x2.shape[0]
    tile = lambda i: (i, 0)
    const = lambda i: (0, 0)
    bf = lambda: jax.ShapeDtypeStruct((n, NA_WIDTH), BF16)
    spec512 = pl.BlockSpec((TM, NA_WIDTH), tile)
    out_shape = [bf(), bf(), bf()]
    out_specs = [spec512, spec512, spec512]
    if is_ctx:
        out_shape += [bf(), jax.ShapeDtypeStruct((n, NA_WIDTH), F32), jax.ShapeDtypeStruct((n, NA_WIDTH), F32)]
        out_specs += [spec512, spec512, spec512]
    else:
        rows = n // batch // GRID_W
        tiles_per_batch = n // batch // TM
        pad_tile = TM // GRID_W * ROW_PITCH
        out_shape += [jax.ShapeDtypeStruct((batch, FN_WIDTH // LANES, rows * ROW_PITCH, LANES), F32)]
        out_specs += [pl.BlockSpec((None, FN_WIDTH // LANES, pad_tile, LANES),
                                   lambda i: (i // tiles_per_batch, 0, i % tiles_per_batch, 0))]
    return pl.pallas_call(
        functools.partial(_inproj_kernel, is_ctx=is_ctx),
        grid=(n // TM,),
        in_specs=[
            pl.BlockSpec((TM, D_MODEL), tile),
            _mods_spec(0, row_of_tile),
            pl.BlockSpec((1, D_MODEL), const),
            pl.BlockSpec((D_MODEL, 4 * NA_WIDTH), const),
            pl.BlockSpec((1, NA_WIDTH), const),
            pl.BlockSpec((1, NA_WIDTH), const),
            pl.BlockSpec((NA_WIDTH, NA_WIDTH), const),
        ],
        out_specs=out_specs,
        out_shape=out_shape,
        name="inproj_ctx" if is_ctx else "inproj_lat",
    )(x2, mods4, g, w_in, qg, kg, bd)


def _softmax_pv(parts):
    m = parts[0][0].max(axis=-1, keepdims=True)
    for s, _ in parts[1:]:
        m = jnp.maximum(m, s.max(axis=-1, keepdims=True))
    l = None
    o = None
    for s, v in parts:
        e = jnp.exp(s - m)
        ls = e.sum(axis=-1, keepdims=True)
        pv = jnp.dot(e.astype(BF16), v, preferred_element_type=F32)
        l = ls if l is None else l + ls
        o = pv if o is None else o + pv
    return o / l


def _nt_dot(a, b):
    return lax.dot_general(a, b, (((1,), (1,)), ((), ())), preferred_element_type=F32)


def _attn_ctx_kernel(q_ref, k_ref, v_ref, o_ref):
    lane = lax.broadcasted_iota(jnp.int32, (q_ref.shape[0], LANES), 1)
    even = lane < HEAD_DIM
    for p in range(NA_HEADS // 2):
        sl = slice(p * LANES, (p + 1) * LANES)
        qp, kp, vp = q_ref[:, sl], k_ref[:, sl], v_ref[:, sl]
        outs = []
        for half in range(2):
            qm = jnp.where(even if half == 0 else ~even, qp, jnp.zeros_like(qp))
            outs.append(_softmax_pv([(_nt_dot(qm, kp), vp)]))
        o_ref[:, sl] = jnp.where(even, outs[0], outs[1]).astype(BF16)


def _attn_ctx(q, k, v, batch):
    n = q.shape[0]
    l = n // batch
    spec = pl.BlockSpec((l, NA_WIDTH), lambda b: (b, 0))
    return pl.pallas_call(
        _attn_ctx_kernel,
        grid=(batch,),
        in_specs=[spec, spec, spec],
        out_specs=spec,
        out_shape=jax.ShapeDtypeStruct((n, NA_WIDTH), BF16),
        name="attn_ctx",
    )(q, k, v)


def _attn_lat_kernel(q_ref, k_ref, v_ref, kc_ref, vc_ref, bias_ref, o_ref, *, rows):
    g = pl.program_id(1)
    base = jnp.clip(ROW_GROUP * g - WIN_ROWS // 2, 0, rows - GROUP_WIN_ROWS)
    start = pl.multiple_of(base * GRID_W, GRID_W)
    win = pl.ds(start, GROUP_WIN_ROWS * GRID_W)
    lane = lax.broadcasted_iota(jnp.int32, (ROW_GROUP * GRID_W, LANES), 1)
    even = lane < HEAD_DIM
    for p in range(NA_HEADS // 2):
        sl = slice(p * LANES, (p + 1) * LANES)
        qp = q_ref[:, sl]
        kw, vw = k_ref[win, sl], v_ref[win, sl]
        kc, vc = kc_ref[:, sl], vc_ref[:, sl]
        outs = []
        for half in range(2):
            qm = jnp.where(even if half == 0 else ~even, qp, jnp.zeros_like(qp))
            s_lat = _nt_dot(qm, kw) + bias_ref[2 * p + half]
            s_ctx = _nt_dot(qm, kc)
            outs.append(_softmax_pv([(s_lat, vw), (s_ctx, vc)]))
        o_ref[:, sl] = jnp.where(even, outs[0], outs[1]).astype(BF16)


def _attn_lat(q, k, v, kc, vc, bias_tbl, batch):
    n = q.shape[0]
    l = n // batch
    rows = l // GRID_W
    groups = rows // ROW_GROUP
    q3, k3, v3 = (t.reshape(batch, l, NA_WIDTH) for t in (q, k, v))
    full = pl.BlockSpec((None, l, NA_WIDTH), lambda b, g: (b, 0, 0))
    ctx = pl.BlockSpec((None, kc.shape[1], NA_WIDTH), lambda b, g: (b, 0, 0))
    qspec = pl.BlockSpec((None, ROW_GROUP * GRID_W, NA_WIDTH), lambda b, g: (b, g, 0))
    bias_class = lambda b, g: (jnp.where(g == 0, 0, jnp.where(g == groups - 1, 2, 1)), 0, 0, 0)
    out = pl.pallas_call(
        functools.partial(_attn_lat_kernel, rows=rows),
        grid=(batch, groups),
        in_specs=[
            qspec, full, full, ctx, ctx,
            pl.BlockSpec((None, NA_HEADS, ROW_GROUP * GRID_W, GROUP_WIN_ROWS * GRID_W), bias_class),
        ],
        out_specs=qspec,
        out_shape=jax.ShapeDtypeStruct((batch, l, NA_WIDTH), BF16),
        compiler_params=pltpu.CompilerParams(vmem_limit_bytes=VMEM_LIMIT),
        name="attn_lat",
    )(q3, k3, v3, kc, vc, bias_tbl)
    return out.reshape(n, NA_WIDTH)


def _bias_table(rpb_i, rows):
    groups = rows // ROW_GROUP
    assert rows % ROW_GROUP == 0 and rows >= GROUP_WIN_ROWS and groups >= 3
    col = np.arange(GRID_W)
    dc = np.clip(col[None, :] - col[:, None] + (WIN_COLS - 1), 0, 2 * WIN_COLS - 2)
    col_start = np.clip(col - WIN_COLS // 2, 0, GRID_W - WIN_COLS)
    col_valid = (col[None, :] >= col_start[:, None]) & (col[None, :] < col_start[:, None] + WIN_COLS)
    col_onehot = (dc[:, :, None] == np.arange(2 * WIN_COLS - 1)).astype(np.float32)
    row_onehot = np.zeros((3, ROW_GROUP, GROUP_WIN_ROWS, 2 * WIN_ROWS - 1), np.float32)
    row_valid = np.zeros((3, ROW_GROUP, GROUP_WIN_ROWS), bool)
    for cls, g in enumerate((0, 1, groups - 1)):
        base = np.clip(ROW_GROUP * g - WIN_ROWS // 2, 0, rows - GROUP_WIN_ROWS)
        for j in range(ROW_GROUP):
            r = ROW_GROUP * g + j
            row_start = np.clip(r - WIN_ROWS // 2, 0, rows - WIN_ROWS)
            for u in range(GROUP_WIN_ROWS):
                if row_start <= base + u < row_start + WIN_ROWS:
                    row_onehot[cls, j, u, base + u - r + WIN_ROWS - 1] = 1.0
                    row_valid[cls, j, u] = True
    t = jnp.einsum("hrc,xjur,qkc->xhjquk", rpb_i.astype(F32), row_onehot, col_onehot,
                   precision=lax.Precision.HIGHEST)
    valid = row_valid[:, None, :, None, :, None] & col_valid[None, None, None, :, None, :]
    t = jnp.where(valid, t, NEG_INF)
    return t.reshape(3, NA_HEADS, ROW_GROUP * GRID_W, GROUP_WIN_ROWS * GRID_W)


def _bf16_table(a):
    return jnp.asarray(a, F32).astype(BF16)


def _dft_cos_sin(n):
    idx = np.arange(n)
    ang = 2.0 * np.pi * ((idx[:, None] * idx[None, :]) % n) / n
    return np.cos(ang), np.sin(ang)


def _channel_dft(width, scale):
    c, s = _dft_cos_sin(FN_GROUP_DIM)
    eye = np.eye(width // FN_GROUP_DIM)
    return np.concatenate([np.kron(eye, c), np.kron(eye, s)], axis=0) * scale


def _fft_ctx_kernel(u_ref, fc_ref, cs_ref, o_ref):
    l = u_ref.shape[0]
    x = jnp.dot(fc_ref[...], u_ref[...], preferred_element_type=F32)
    xri = jnp.concatenate([x[:l], x[l:]], axis=1).astype(BF16)
    o_ref[...] = jnp.dot(xri, cs_ref[...], preferred_element_type=F32).astype(BF16)


def _fft_ctx(u, batch):
    n = u.shape[0]
    l = n // batch
    c, s = _dft_cos_sin(l)
    fc = _bf16_table(np.concatenate([c, -s], axis=0))
    cs = _bf16_table(_channel_dft(FN_WIDTH, (l * FN_GROUP_DIM) ** -0.5))
    spec = pl.BlockSpec((l, FN_WIDTH), lambda b: (b, 0))
    return pl.pallas_call(
        _fft_ctx_kernel,
        grid=(batch,),
        in_specs=[spec, pl.BlockSpec(fc.shape, lambda b: (0, 0)), pl.BlockSpec(cs.shape, lambda b: (0, 0))],
        out_specs=spec,
        out_shape=jax.ShapeDtypeStruct((n, FN_WIDTH), BF16),
        name="fourier_ctx",
    )(u, fc, cs)


def _fft_lat_kernel(u_ref, f1_ref, twr_ref, twi_ref, f2_ref, cs_ref, o_ref, z_ref, x2_ref):
    slabs = FFT_CB // LANES
    n1 = GRID_W
    x2_ref[...] = jnp.zeros(x2_ref.shape, F32)

    def stage1(n2, carry):
        u = jnp.concatenate([u_ref[j, pl.ds(n2, n1, stride=ROW_PITCH), :] for j in range(slabs)], axis=1)
        y = jnp.dot(f1_ref[...], u.astype(BF16), preferred_element_type=F32)
        yr, yi = y[:n1], y[n1:]
        tr = jnp.concatenate([twr_ref[n2]] * slabs, axis=1)
        ti = jnp.concatenate([twi_ref[n2]] * slabs, axis=1)
        zr = yr * tr - yi * ti
        zi = yr * ti + yi * tr
        for j in range(slabs):
            sl = slice(j * LANES, (j + 1) * LANES)
            z_ref[0, j, pl.ds(n2, n1, stride=ROW_PITCH), :] = zr[:, sl]
            z_ref[1, j, pl.ds(n2, n1, stride=ROW_PITCH), :] = zi[:, sl]
        return carry

    lax.fori_loop(0, GRID_W, stage1, 0, unroll=FFT_UNROLL)

    def stage2(k1, carry):
        base = pl.multiple_of(k1 * ROW_PITCH, 8)
        zr = jnp.concatenate([z_ref[0, j, pl.ds(base, GRID_W), :] for j in range(slabs)], axis=1)
        zi = jnp.concatenate([z_ref[1, j, pl.ds(base, GRID_W), :] for j in range(slabs)], axis=1)
        zz = jnp.concatenate([zr, zi], axis=0).astype(BF16)
        x = jnp.dot(f2_ref[...], zz, preferred_element_type=F32)
        for j in range(slabs):
            sl = slice(j * LANES, (j + 1) * LANES)
            x2_ref[j, pl.ds(k1, GRID_W, stride=ROW_PITCH), :] = x[:GRID_W, sl]
            x2_ref[slabs + j, pl.ds(k1, GRID_W, stride=ROW_PITCH), :] = x[GRID_W:, sl]
        return carry

    lax.fori_loop(0, n1, stage2, 0, unroll=FFT_UNROLL)

    chunk = 8 * ROW_PITCH
    for c in range(GRID_W * ROW_PITCH // chunk):
        rs = slice(c * chunk, (c + 1) * chunk)
        xri = jnp.concatenate([x2_ref[s, rs, :] for s in range(2 * slabs)], axis=1).astype(BF16)
        o_ref[rs, :] = jnp.dot(xri, cs_ref[...], preferred_element_type=F32)


def _fft_lat(u_pad, batch):
    l = GRID_W * GRID_W
    rows_pad = GRID_W * ROW_PITCH
    slabs = FFT_CB // LANES
    c, s = _dft_cos_sin(GRID_W)
    f1 = _bf16_table(np.concatenate([c, -s], axis=0))
    f2 = _bf16_table(np.block([[c, s], [-s, c]]))
    n2k1 = (np.arange(GRID_W)[:, None] * np.arange(GRID_W)[None, :]) % l
    ang = 2.0 * np.pi * n2k1 / l
    twr = jnp.broadcast_to(jnp.asarray(np.cos(ang), F32)[:, :, None], (GRID_W, GRID_W, LANES))
    twi = jnp.broadcast_to(jnp.asarray(-np.sin(ang), F32)[:, :, None], (GRID_W, GRID_W, LANES))
    cs = _bf16_table(_channel_dft(FFT_CB, (l * FN_GROUP_DIM) ** -0.5))
    const2 = lambda b, cb: (0, 0)
    const3 = lambda b, cb: (0, 0, 0)
    return pl.pallas_call(
        _fft_lat_kernel,
        grid=(batch, FN_WIDTH // FFT_CB),
        in_specs=[
            pl.BlockSpec((None, slabs, rows_pad, LANES), lambda b, cb: (b, cb, 0, 0)),
            pl.BlockSpec(f1.shape, const2),
            pl.BlockSpec(twr.shape, const3),
            pl.BlockSpec(twi.shape, const3),
            pl.BlockSpec(f2.shape, const2),
            pl.BlockSpec(cs.shape, const2),
        ],
        out_specs=pl.BlockSpec((None, rows_pad, FFT_CB), lambda b, cb: (b, 0, cb)),
        out_shape=jax.ShapeDtypeStruct((batch, rows_pad, FN_WIDTH), F32),
        scratch_shapes=[
            pltpu.VMEM((2, slabs, rows_pad, LANES), F32),
            pltpu.VMEM((2 * slabs, rows_pad, LANES), F32),
        ],
        compiler_params=pltpu.CompilerParams(vmem_limit_bytes=VMEM_LIMIT),
        name="fourier_lat",
    )(u_pad, f1, twr, twi, f2, cs)


def _outproj_kernel(x_ref, a_ref, f_ref, m_ref, w_ref, o_ref, *, is_ctx):
    if is_ctx:
        f = f_ref[...]
    else:
        f = jnp.concatenate(
            [f_ref[r * ROW_PITCH:r * ROW_PITCH + GRID_W, :] for r in range(TM // GRID_W)], axis=0).astype(BF16)
    af = jnp.concatenate([a_ref[...], f], axis=1)
    mix = jnp.dot(af, w_ref[...], preferred_element_type=F32)
    o_ref[...] = x_ref[...] + _mod_chunk(m_ref, 2) * mix


def _outproj(x2, a, f, mods4, row_of_tile, w_out, is_ctx, batch):
    n = x2.shape[0]
    tile = lambda i: (i, 0)
    if is_ctx:
        fspec = pl.BlockSpec((TM, FN_WIDTH), tile)
    else:
        tiles_per_batch = n // batch // TM
        fspec = pl.BlockSpec((None, TM // GRID_W * ROW_PITCH, FN_WIDTH),
                             lambda i: (i // tiles_per_batch, i % tiles_per_batch, 0))
    return pl.pallas_call(
        functools.partial(_outproj_kernel, is_ctx=is_ctx),
        grid=(n // TM,),
        in_specs=[
            pl.BlockSpec((TM, D_MODEL), tile),
            pl.BlockSpec((TM, NA_WIDTH), tile),
            fspec,
            _mods_spec(0, row_of_tile),
            pl.BlockSpec((D_MODEL, D_MODEL), lambda i: (0, 0)),
        ],
        out_specs=pl.BlockSpec((TM, D_MODEL), tile),
        out_shape=jax.ShapeDtypeStruct((n, D_MODEL), F32),
        name="outproj_ctx" if is_ctx else "outproj_lat",
    )(x2, a, f, mods4, w_out)


def _sg_kernel(x_ref, m_ref, g_ref, win_ref, vg_ref, ws_ref, bs_ref, wout_ref, o_ref):
    x = x_ref[...]
    h = _rms_mod(x, g_ref[...], _mod_chunk(m_ref, 0), _mod_chunk(m_ref, 1)).astype(BF16)
    z = jax.nn.gelu(jnp.dot(h, win_ref[...], preferred_element_type=F32))
    gated = []
    for g in range(SG_GROUPS):
        gs = slice(g * SG_GROUP_DIM, (g + 1) * SG_GROUP_DIM)
        u = z[:, gs]
        v = z[:, SG_WIDTH + g * SG_GROUP_DIM:SG_WIDTH + (g + 1) * SG_GROUP_DIM]
        ms = jnp.mean(v * v, axis=-1, keepdims=True)
        vn = (v * lax.rsqrt(ms + EPS) * vg_ref[:, gs]).astype(BF16)
        w = ws_ref[g]
        s = jnp.concatenate(
            [jnp.dot(w, vn[c * CHUNK:(c + 1) * CHUNK], preferred_element_type=F32) for c in range(TM // CHUNK)],
            axis=0)
        bias = jnp.concatenate([bs_ref[:, gs]] * (TM // CHUNK), axis=0)
        gated.append((u * (s + bias)).astype(BF16))
    mix = jnp.dot(jnp.concatenate(gated, axis=1), wout_ref[...], preferred_element_type=F32)
    o_ref[...] = x + _mod_chunk(m_ref, 2) * mix


def _spatial_gating(x2, mods4, row_of_tile, g, w_in, v_gain, w_s, b_full, w_out):
    n = x2.shape[0]
    tile = lambda i: (i, 0)
    const = lambda i: (0, 0)
    return pl.pallas_call(
        _sg_kernel,
        grid=(n // TM,),
        in_specs=[
            pl.BlockSpec((TM, D_MODEL), tile),
            _mods_spec(1, row_of_tile),
            pl.BlockSpec((1, D_MODEL), const),
            pl.BlockSpec((D_MODEL, 2 * SG_WIDTH), const),
            pl.BlockSpec((1, SG_WIDTH), const),
            pl.BlockSpec((SG_GROUPS, CHUNK, CHUNK), lambda i: (0, 0, 0)),
            pl.BlockSpec((CHUNK, SG_WIDTH), const),
            pl.BlockSpec((SG_WIDTH, D_MODEL), const),
        ],
        out_specs=pl.BlockSpec((TM, D_MODEL), tile),
        out_shape=jax.ShapeDtypeStruct((n, D_MODEL), F32),
        name="spatial_gating",
    )(x2, mods4, g, w_in, v_gain, w_s, b_full, w_out)


def _first_index_of(mask, idx, size):
    return jnp.min(jnp.where(mask, idx, size), axis=0, keepdims=True)


def _route(h, h_hi, rw_ref, rb_ref):
    h_lo = (h - h_hi.astype(F32)).astype(BF16)
    rw = rw_ref[...]
    rw_hi = rw.astype(BF16)
    rw_lo = (rw - rw_hi.astype(F32)).astype(BF16)
    hi_both = jnp.dot(h_hi, jnp.concatenate([rw_hi, rw_lo], axis=1), preferred_element_type=F32)
    logits = (hi_both[:, :LANES] + jnp.dot(h_lo, rw_hi, preferred_element_type=F32)
              + hi_both[:, LANES:])
    logits = logits.T[:N_EXPERTS]
    scores = jax.nn.sigmoid(logits)
    sel = scores + rb_ref[...]
    eg = EXPERTS_PER_GROUP
    ri = lax.broadcasted_iota(jnp.int32, (eg, TM), 0)
    firsts, seconds, group_scores = [], [], []
    for g in range(N_EXPERT_GROUPS):
        a = sel[g * eg:(g + 1) * eg]
        m1 = a.max(axis=0, keepdims=True)
        i1 = _first_index_of(a == m1, ri, eg)
        rest = jnp.where(ri == i1, -jnp.inf, a)
        m2 = rest.max(axis=0, keepdims=True)
        i2 = _first_index_of(rest == m2, ri, eg)
        firsts.append(i1)
        seconds.append(i2)
        group_scores.append(m1 + m2)
    gs = jnp.concatenate(group_scores, axis=0)
    gi = lax.broadcasted_iota(jnp.int32, gs.shape, 0)
    g_best = _first_index_of(gs == gs.max(axis=0, keepdims=True), gi, N_EXPERT_GROUPS)
    picked = []
    for g in range(N_EXPERT_GROUPS):
        chosen = (g_best == g) & ((ri == firsts[g]) | (ri == seconds[g]))
        picked.append(jnp.where(chosen, scores[g * eg:(g + 1) * eg], 0.0))
    w = jnp.concatenate(picked, axis=0)
    return w / w.sum(axis=0, keepdims=True), g_best


def _split_bf16(t):
    hi = t.astype(BF16)
    return hi, (t - hi.astype(F32)).astype(BF16)


def _tn_dot(a, b):
    return lax.dot_general(a, b, (((0,), (0,)), ((), ())), preferred_element_type=F32)


def _moe_kernel(x_ref, m_ref, g_ref, rw_ref, rb_ref, tri_ref, wg_ref, wu_ref, wd_ref, o_ref):
    x = x_ref[...]
    h = _rms_mod(x, g_ref[...], _mod_chunk(m_ref, 3), _mod_chunk(m_ref, 4))
    h_hi = h.astype(BF16)
    w, g_best = _route(h, h_hi, rw_ref, rb_ref)
    wt = jnp.concatenate([w, jnp.zeros((LANES - N_EXPERTS, TM), F32)], axis=0)
    gates_hi, gates_lo = _split_bf16(wt.T)

    gi = lax.broadcasted_iota(jnp.int32, (2 * N_EXPERT_GROUPS, TM), 0)
    onehot = jnp.where(gi == g_best, 1.0, 0.0)
    prefix = jnp.dot(onehot.astype(BF16), tri_ref[...], preferred_element_type=F32)
    rank = jnp.sum(onehot * prefix, axis=0, keepdims=True)
    slot = lax.broadcasted_iota(jnp.int32, (MOE_CHUNK, TM), 0).astype(F32)

    o_ref[...] = x
    gate2 = _mod_chunk(m_ref, 5)
    eg = EXPERTS_PER_GROUP
    for g in range(N_EXPERT_GROUPS):
        count = jnp.sum(onehot[g:g + 1]).astype(jnp.int32)

        def chunk(j, carry, g=g):
            key = jnp.where(g_best == g, rank - (j * MOE_CHUNK).astype(F32), -1.0)
            perm = jnp.where(slot == key, 1.0, 0.0).astype(BF16)
            hs = jnp.dot(perm, h_hi, preferred_element_type=F32).astype(BF16)
            gs = (jnp.dot(perm, gates_hi, preferred_element_type=F32)
                  + jnp.dot(perm, gates_lo, preferred_element_type=F32))
            y = jnp.zeros((MOE_CHUNK, D_MODEL), F32)
            for e in range(g * eg, (g + 1) * eg):
                a = jnp.dot(hs, wg_ref[e], preferred_element_type=F32)
                b = jnp.dot(hs, wu_ref[e], preferred_element_type=F32)
                hid = jax.nn.silu(a) * b * gs[:, e:e + 1]
                y = y + jnp.dot(hid.astype(BF16), wd_ref[e], preferred_element_type=F32)
            o_ref[...] += _tn_dot(perm, (gate2 * y).astype(BF16))
            return carry

        lax.fori_loop(0, (count + MOE_CHUNK - 1) // MOE_CHUNK, chunk, 0)


def _moe(x2, mods4, layer, row_of_tile, g, rw, rb, wg, wu, wd):
    n = x2.shape[0]
    tile = lambda i: (i, 0)
    const = lambda i: (0, 0)
    resident = lambda shape: pl.BlockSpec(shape, lambda i: (0, 0, 0), pipeline_mode=pl.Buffered(1))
    tri = jnp.asarray(np.triu(np.ones((TM, TM), np.float32), k=1), BF16)
    return pl.pallas_call(
        _moe_kernel,
        grid=(n // TM,),
        in_specs=[
            pl.BlockSpec((TM, D_MODEL), tile),
            _mods_spec(layer, row_of_tile),
            pl.BlockSpec((1, D_MODEL), const),
            pl.BlockSpec((D_MODEL, LANES), const),
            pl.BlockSpec((N_EXPERTS, 1), const),
            pl.BlockSpec((TM, TM), const),
            resident(wg.shape), resident(wu.shape), resident(wd.shape),
        ],
        out_specs=pl.BlockSpec((TM, D_MODEL), tile),
        out_shape=jax.ShapeDtypeStruct((n, D_MODEL), F32),
        compiler_params=pltpu.CompilerParams(vmem_limit_bytes=VMEM_LIMIT),
        name="moe",
    )(x2, mods4, g, rw, rb, tri, wg, wu, wd)


def _trunk(x, mods4, row_of_tile, cache, p, is_ctx):
    batch, l, _ = x.shape
    x2 = x.reshape(batch * l, D_MODEL)

    outs = _inproj(x2, mods4, row_of_tile, p["norm1_g"][0], p["na_w_in"], p["q_gain"], p["k_gain"], p["head_bd"],
                   is_ctx, batch)
    if is_ctx:
        q, k, v, u, k32, v32 = outs
        a = _attn_ctx(q, k, v, batch)
        f = _fft_ctx(u, batch)
    else:
        q, k, v, u_pad = outs
        a = _attn_lat(q, k, v, cache[0], cache[1], p["bias_tbl"], batch)
        f = _fft_lat(u_pad, batch)
        k32 = v32 = None
    x2 = _outproj(x2, a, f, mods4, row_of_tile, p["na_w_out"], is_ctx, batch)
    x2 = _moe(x2, mods4, 0, row_of_tile, p["norm2_g"][0], p["router_w"], p["router_b"],
              p["moe_wg"][0], p["moe_wu"][0], p["moe_wd"][0])

    x2 = _spatial_gating(x2, mods4, row_of_tile, p["norm1_g"][1], p["sg_w_in"], p["sg_v_gain"], p["sg_w_s"],
                         p["sg_b_full"], p["sg_w_out"])
    x2 = _moe(x2, mods4, 1, row_of_tile, p["norm2_g"][1], p["router_w"], p["router_b"],
              p["moe_wg"][1], p["moe_wu"][1], p["moe_wd"][1])
    return x2.reshape(batch, l, D_MODEL), k32, v32


def kernel(x_prompt, x_sample, cache_k, cache_v, c, c_ctx, ada_w, ada_b, norm1_g, norm2_g, na_w_in, na_w_out,
           q_gain, k_gain, rpb, sg_w_in, sg_w_out, sg_v_gain, sg_w_s, sg_b_s, router_w, router_b,
           moe_w_gate, moe_w_up, moe_w_down):
    batch, seq, _ = x_prompt.shape
    dec_batch, dec_seq, _ = x_sample.shape
    assert dec_batch <= CTX_ROW and dec_seq == GRID_W * GRID_W and (batch * seq) % TM == 0
    assert ada_w.shape[0] == 2 and na_w_in.shape[0] == 1 and sg_w_in.shape[0] == 1

    cond = jnp.zeros((N_COND_ROWS, D_MODEL), F32).at[:dec_batch].set(c).at[CTX_ROW].set(c_ctx)
    mods = _ada_table(cond, ada_w, ada_b)
    mods4 = mods.reshape(mods.shape[0], N_COND_ROWS, 1, mods.shape[2])

    tile_rep = lambda t: jnp.tile(t, NA_HEADS).reshape(1, NA_WIDTH)
    p = {
        "norm1_g": norm1_g.reshape(-1, 1, D_MODEL),
        "norm2_g": norm2_g.reshape(-1, 1, D_MODEL),
        "na_w_in": na_w_in[0].astype(BF16),
        "na_w_out": na_w_out[0].astype(BF16),
        "q_gain": tile_rep(q_gain[0]),
        "k_gain": tile_rep(k_gain[0]),
        "head_bd": jnp.asarray(np.kron(np.eye(NA_HEADS), np.full((HEAD_DIM, HEAD_DIM), 1.0 / HEAD_DIM)), BF16),
        "bias_tbl": _bias_table(rpb[0], dec_seq // GRID_W),
        "sg_w_in": sg_w_in[0].astype(BF16),
        "sg_w_out": sg_w_out[0].astype(BF16),
        "sg_v_gain": sg_v_gain[0].reshape(1, SG_WIDTH),
        "sg_w_s": sg_w_s[0].astype(BF16),
        "sg_b_full": jnp.broadcast_to(sg_b_s[0].T[:, :, None], (CHUNK, SG_GROUPS, SG_GROUP_DIM)).reshape(
            CHUNK, SG_WIDTH),
        "router_w": jnp.pad(router_w, ((0, 0), (0, LANES - N_EXPERTS))),
        "router_b": router_b.reshape(N_EXPERTS, 1),
        "moe_wg": moe_w_gate.astype(BF16),
        "moe_wu": moe_w_up.astype(BF16),
        "moe_wd": moe_w_down.astype(BF16),
    }

    y_prompt, k32, v32 = _trunk(x_prompt, mods4, lambda i: CTX_ROW, None, p, True)
    k_ctx = k32.reshape(batch, 1, seq, NA_HEADS, HEAD_DIM)
    v_ctx = v32.reshape(batch, 1, seq, NA_HEADS, HEAD_DIM)

    tiles_per_batch = dec_seq // TM
    cache = (cache_k[:, 0].reshape(dec_batch, -1, NA_WIDTH).astype(BF16),
             cache_v[:, 0].reshape(dec_batch, -1, NA_WIDTH).astype(BF16))
    y_sample, _, _ = _trunk(x_sample, mods4, lambda i: i // tiles_per_batch, cache, p, False)
    return (y_prompt, y_sample, k_ctx, v_ctx)
```

```python
import functools

import numpy as np
import jax
import jax.numpy as jnp
from jax import lax
from jax.experimental import pallas as pl
from jax.experimental.pallas import tpu as pltpu

F32 = jnp.float32
BF16 = jnp.bfloat16

D_MODEL = 1024
GRID_W = 64
HEAD_DIM = 64
NA_WIDTH = 512
NA_HEADS = 8
WIN_ROWS = 8
WIN_COLS = 16
FN_WIDTH = 512
FN_GROUP_DIM = 64
SG_WIDTH = 1024
SG_GROUPS = 4
SG_GROUP_DIM = 256
CHUNK = 128
N_EXPERTS = 16
N_EXPERT_GROUPS = 4
EXPERTS_PER_GROUP = 4
EXPERT_FF = 256
EPS = 1e-6
NEG_INF = -1e30
LOG2_E = 1.4426950408889634

LANES = 128
TM = 512
MOE_CHUNK = 160
ROW_GROUP = 4
GROUP_WIN_ROWS = 12
ROW_PITCH = 72
N_COND_ROWS = 8
CTX_ROW = 4
ADA_TN = 1536
FFT_UNROLL = 8
FFT_CB = 256
VMEM_LIMIT = 56 * 1024 * 1024


def _rms_mod(x, g, shift, scale):
    ms = jnp.mean(x * x, axis=-1, keepdims=True)
    y = x * lax.rsqrt(ms + EPS) * g
    return y * (1.0 + scale) + shift


def _mod_chunk(m_ref, i):
    return m_ref[:, i * D_MODEL:(i + 1) * D_MODEL]


def _ada_kernel(c_ref, w_ref, b_ref, o_ref):
    s = jax.nn.silu(c_ref[...]).astype(BF16)
    o_ref[...] = jnp.dot(s, w_ref[...].astype(BF16), preferred_element_type=F32) + b_ref[...]


def _ada_table(cond, ada_w, ada_b):
    depth = ada_w.shape[0]
    n_out = ada_w.shape[2]
    return pl.pallas_call(
        _ada_kernel,
        grid=(depth, n_out // ADA_TN),
        in_specs=[
            pl.BlockSpec((N_COND_ROWS, D_MODEL), lambda l, j: (0, 0)),
            pl.BlockSpec((None, D_MODEL, ADA_TN), lambda l, j: (l, 0, j)),
            pl.BlockSpec((None, 1, ADA_TN), lambda l, j: (l, 0, j)),
        ],
        out_specs=pl.BlockSpec((None, N_COND_ROWS, ADA_TN), lambda l, j: (l, 0, j)),
        out_shape=jax.ShapeDtypeStruct((depth, N_COND_ROWS, n_out), F32),
        name="ada_table",
    )(cond, ada_w, ada_b.reshape(depth, 1, n_out))


def _head_norm(t, bd_ref, gain):
    ms = jnp.dot((t * t).astype(BF16), bd_ref[...], preferred_element_type=F32)
    return t * lax.rsqrt(ms + EPS) * gain


def _inproj_kernel(x_ref, m_ref, g_ref, w_ref, qg_ref, kg_ref, bd_ref, *out_refs, is_ctx):
    h = _rms_mod(x_ref[...], g_ref[...], _mod_chunk(m_ref, 0), _mod_chunk(m_ref, 1)).astype(BF16)
    proj = jnp.dot(h, w_ref[...], preferred_element_type=F32)
    q = _head_norm(proj[:, 0:NA_WIDTH], bd_ref, qg_ref[...]) * (HEAD_DIM ** -0.5 * LOG2_E)
    k = _head_norm(proj[:, NA_WIDTH:2 * NA_WIDTH], bd_ref, kg_ref[...])
    v = proj[:, 2 * NA_WIDTH:3 * NA_WIDTH]
    u = proj[:, 3 * NA_WIDTH:]
    if is_ctx:
        q_ref, k_ref, v_ref, u_ref, k32_ref, v32_ref = out_refs
        k32_ref[...] = k
        v32_ref[...] = v
        u_ref[...] = u.astype(BF16)
    else:
        q_ref, k_ref, v_ref, u_ref = out_refs
        zeros = jnp.zeros((ROW_PITCH - GRID_W, LANES), F32)
        for r in range(TM // GRID_W):
            for j in range(FN_WIDTH // LANES):
                u_ref[j, r * ROW_PITCH:r * ROW_PITCH + GRID_W, :] = (
                    u[r * GRID_W:(r + 1) * GRID_W, j * LANES:(j + 1) * LANES])
                u_ref[j, r * ROW_PITCH + GRID_W:(r + 1) * ROW_PITCH, :] = zeros
    q_ref[...] = q.astype(BF16)
    k_ref[...] = k.astype(BF16)
    v_ref[...] = v.astype(BF16)


def _mods_spec(layer, row_of_tile):
    return pl.BlockSpec((None, None, 1, 6 * D_MODEL), lambda i: (layer, row_of_tile(i), 0, 0))


def _inproj(x2, mods4, row_of_tile, g, w_in, qg, kg, bd, is_ctx, batch):
    n = x2.shape[0]
    tile = lambda i: (i, 0)
    const = lambda i: (0, 0)
    bf = lambda: jax.ShapeDtypeStruct((n, NA_WIDTH), BF16)
    spec512 = pl.BlockSpec((TM, NA_WIDTH), tile)
    out_shape = [bf(), bf(), bf()]
    out_specs = [spec512, spec512, spec512]
    if is_ctx:
        out_shape += [bf(), jax.ShapeDtypeStruct((n, NA_WIDTH), F32), jax.ShapeDtypeStruct((n, NA_WIDTH), F32)]
        out_specs += [spec512, spec512, spec512]
    else:
        rows = n // batch // GRID_W
        tiles_per_batch = n // batch // TM
        pad_tile = TM // GRID_W * ROW_PITCH
        out_shape += [jax.ShapeDtypeStruct((batch, FN_WIDTH // LANES, rows * ROW_PITCH, LANES), F32)]
        out_specs += [pl.BlockSpec((None, FN_WIDTH // LANES, pad_tile, LANES),
                                   lambda i: (i // tiles_per_batch, 0, i % tiles_per_batch, 0))]
    return pl.pallas_call(
        functools.partial(_inproj_kernel, is_ctx=is_ctx),
        grid=(n // TM,),
        in_specs=[
            pl.BlockSpec((TM, D_MODEL), tile),
            _mods_spec(0, row_of_tile),
            pl.BlockSpec((1, D_MODEL), const),
            pl.BlockSpec((D_MODEL, 4 * NA_WIDTH), const),
            pl.BlockSpec((1, NA_WIDTH), const),
            pl.BlockSpec((1, NA_WIDTH), const),
            pl.BlockSpec((NA_WIDTH, NA_WIDTH), const),
        ],
        out_specs=out_specs,
        out_shape=out_shape,
        name="inproj_ctx" if is_ctx else "inproj_lat",
    )(x2, mods4, g, w_in, qg, kg, bd)


def _softmax_pv(parts):
    m = parts[0][0].max(axis=-1, keepdims=True)
    for s, _ in parts[1:]:
        m = jnp.maximum(m, s.max(axis=-1, keepdims=True))
    l = None
    o = None
    for s, v in parts:
        e = jnp.exp2(s - m)
        ls = e.sum(axis=-1, keepdims=True)
        pv = jnp.dot(e.astype(BF16), v, preferred_element_type=F32)
        l = ls if l is None else l + ls
        o = pv if o is None else o + pv
    return o / l


def _nt_dot(a, b):
    return lax.dot_general(a, b, (((1,), (1,)), ((), ())), preferred_element_type=F32)


def _attn_ctx_kernel(q_ref, k_ref, v_ref, o_ref):
    lane = lax.broadcasted_iota(jnp.int32, (q_ref.shape[0], LANES), 1)
    even = lane < HEAD_DIM
    for p in range(NA_HEADS // 2):
        sl = slice(p * LANES, (p + 1) * LANES)
        qp, kp, vp = q_ref[:, sl], k_ref[:, sl], v_ref[:, sl]
        outs = []
        for half in range(2):
            qm = jnp.where(even if half == 0 else ~even, qp, jnp.zeros_like(qp))
            outs.append(_softmax_pv([(_nt_dot(qm, kp), vp)]))
        o_ref[:, sl] = jnp.where(even, outs[0], outs[1]).astype(BF16)


def _attn_ctx(q, k, v, batch):
    n = q.shape[0]
    l = n // batch
    spec = pl.BlockSpec((l, NA_WIDTH), lambda b: (b, 0))
    return pl.pallas_call(
        _attn_ctx_kernel,
        grid=(batch,),
        in_specs=[spec, spec, spec],
        out_specs=spec,
        out_shape=jax.ShapeDtypeStruct((n, NA_WIDTH), BF16),
        name="attn_ctx",
    )(q, k, v)


def _attn_lat_kernel(q_ref, k_ref, v_ref, kc_ref, vc_ref, bias_ref, o_ref, *, rows):
    g = pl.program_id(1)
    base = jnp.clip(ROW_GROUP * g - WIN_ROWS // 2, 0, rows - GROUP_WIN_ROWS)
    start = pl.multiple_of(base * GRID_W, GRID_W)
    win = pl.ds(start, GROUP_WIN_ROWS * GRID_W)
    lane = lax.broadcasted_iota(jnp.int32, (ROW_GROUP * GRID_W, LANES), 1)
    even = lane < HEAD_DIM
    for p in range(NA_HEADS // 2):
        sl = slice(p * LANES, (p + 1) * LANES)
        qp = q_ref[:, sl]
        kw, vw = k_ref[win, sl], v_ref[win, sl]
        kc, vc = kc_ref[:, sl], vc_ref[:, sl]
        outs = []
        for half in range(2):
            qm = jnp.where(even if half == 0 else ~even, qp, jnp.zeros_like(qp))
            s_lat = _nt_dot(qm, kw) + bias_ref[2 * p + half]
            s_ctx = _nt_dot(qm, kc)
            outs.append(_softmax_pv([(s_lat, vw), (s_ctx, vc)]))
        o_ref[:, sl] = jnp.where(even, outs[0], outs[1]).astype(BF16)


def _attn_lat(q, k, v, kc, vc, bias_tbl, batch):
    n = q.shape[0]
    l = n // batch
    rows = l // GRID_W
    groups = rows // ROW_GROUP
    q3, k3, v3 = (t.reshape(batch, l, NA_WIDTH) for t in (q, k, v))
    full = pl.BlockSpec((None, l, NA_WIDTH), lambda b, g: (b, 0, 0))
    ctx = pl.BlockSpec((None, kc.shape[1], NA_WIDTH), lambda b, g: (b, 0, 0))
    qspec = pl.BlockSpec((None, ROW_GROUP * GRID_W, NA_WIDTH), lambda b, g: (b, g, 0))
    bias_class = lambda b, g: (jnp.where(g == 0, 0, jnp.where(g == groups - 1, 2, 1)), 0, 0, 0)
    out = pl.pallas_call(
        functools.partial(_attn_lat_kernel, rows=rows),
        grid=(batch, groups),
        in_specs=[
            qspec, full, full, ctx, ctx,
            pl.BlockSpec((None, NA_HEADS, ROW_GROUP * GRID_W, GROUP_WIN_ROWS * GRID_W), bias_class),
        ],
        out_specs=qspec,
        out_shape=jax.ShapeDtypeStruct((batch, l, NA_WIDTH), BF16),
        compiler_params=pltpu.CompilerParams(vmem_limit_bytes=VMEM_LIMIT),
        name="attn_lat",
    )(q3, k3, v3, kc, vc, bias_tbl)
    return out.reshape(n, NA_WIDTH)


def _bias_table(rpb_i, rows):
    groups = rows // ROW_GROUP
    assert rows % ROW_GROUP == 0 and rows >= GROUP_WIN_ROWS and groups >= 3
    col = np.arange(GRID_W)
    dc = np.clip(col[None, :] - col[:, None] + (WIN_COLS - 1), 0, 2 * WIN_COLS - 2)
    col_start = np.clip(col - WIN_COLS // 2, 0, GRID_W - WIN_COLS)
    col_valid = (col[None, :] >= col_start[:, None]) & (col[None, :] < col_start[:, None] + WIN_COLS)
    col_onehot = (dc[:, :, None] == np.arange(2 * WIN_COLS - 1)).astype(np.float32)
    row_onehot = np.zeros((3, ROW_GROUP, GROUP_WIN_ROWS, 2 * WIN_ROWS - 1), np.float32)
    row_valid = np.zeros((3, ROW_GROUP, GROUP_WIN_ROWS), bool)
    for cls, g in enumerate((0, 1, groups - 1)):
        base = np.clip(ROW_GROUP * g - WIN_ROWS // 2, 0, rows - GROUP_WIN_ROWS)
        for j in range(ROW_GROUP):
            r = ROW_GROUP * g + j
            row_start = np.clip(r - WIN_ROWS // 2, 0, rows - WIN_ROWS)
            for u in range(GROUP_WIN_ROWS):
                if row_start <= base + u < row_start + WIN_ROWS:
                    row_onehot[cls, j, u, base + u - r + WIN_ROWS - 1] = 1.0
                    row_valid[cls, j, u] = True
    t = jnp.einsum("hrc,xjur,qkc->xhjquk", rpb_i.astype(F32), row_onehot, col_onehot,
                   precision=lax.Precision.HIGHEST)
    valid = row_valid[:, None, :, None, :, None] & col_valid[None, None, None, :, None, :]
    t = jnp.where(valid, t * LOG2_E, NEG_INF)
    return t.reshape(3, NA_HEADS, ROW_GROUP * GRID_W, GROUP_WIN_ROWS * GRID_W)


def _bf16_table(a):
    return jnp.asarray(a, F32).astype(BF16)


def _dft_cos_sin(n):
    idx = np.arange(n)
    ang = 2.0 * np.pi * ((idx[:, None] * idx[None, :]) % n) / n
    return np.cos(ang), np.sin(ang)


def _channel_dft(width, scale):
    c, s = _dft_cos_sin(FN_GROUP_DIM)
    eye = np.eye(width // FN_GROUP_DIM)
    return np.concatenate([np.kron(eye, c), np.kron(eye, s)], axis=0) * scale


def _fft_ctx_kernel(u_ref, fc_ref, cs_ref, o_ref):
    l = u_ref.shape[0]
    x = jnp.dot(fc_ref[...], u_ref[...], preferred_element_type=F32)
    xri = jnp.concatenate([x[:l], x[l:]], axis=1).astype(BF16)
    o_ref[...] = jnp.dot(xri, cs_ref[...], preferred_element_type=F32).astype(BF16)


def _fft_ctx(u, batch):
    n = u.shape[0]
    l = n // batch
    c, s = _dft_cos_sin(l)
    fc = _bf16_table(np.concatenate([c, -s], axis=0))
    cs = _bf16_table(_channel_dft(FN_WIDTH, (l * FN_GROUP_DIM) ** -0.5))
    spec = pl.BlockSpec((l, FN_WIDTH), lambda b: (b, 0))
    return pl.pallas_call(
        _fft_ctx_kernel,
        grid=(batch,),
        in_specs=[spec, pl.BlockSpec(fc.shape, lambda b: (0, 0)), pl.BlockSpec(cs.shape, lambda b: (0, 0))],
        out_specs=spec,
        out_shape=jax.ShapeDtypeStruct((n, FN_WIDTH), BF16),
        name="fourier_ctx",
    )(u, fc, cs)


def _fft_lat_kernel(u_ref, f1_ref, twr_ref, twi_ref, f2_ref, cs_ref, o_ref, z_ref, x2_ref):
    slabs = FFT_CB // LANES
    n1 = GRID_W
    x2_ref[...] = jnp.zeros(x2_ref.shape, F32)

    def stage1(n2, carry):
        u = jnp.concatenate([u_ref[j, pl.ds(n2, n1, stride=ROW_PITCH), :] for j in range(slabs)], axis=1)
        y = jnp.dot(f1_ref[...], u.astype(BF16), preferred_element_type=F32)
        yr, yi = y[:n1], y[n1:]
        tr = jnp.concatenate([twr_ref[n2]] * slabs, axis=1)
        ti = jnp.concatenate([twi_ref[n2]] * slabs, axis=1)
        zr = yr * tr - yi * ti
        zi = yr * ti + yi * tr
        for j in range(slabs):
            sl = slice(j * LANES, (j + 1) * LANES)
            z_ref[0, j, pl.ds(n2, n1, stride=ROW_PITCH), :] = zr[:, sl]
            z_ref[1, j, pl.ds(n2, n1, stride=ROW_PITCH), :] = zi[:, sl]
        return carry

    lax.fori_loop(0, GRID_W, stage1, 0, unroll=FFT_UNROLL)

    def stage2(k1, carry):
        base = pl.multiple_of(k1 * ROW_PITCH, 8)
        zr = jnp.concatenate([z_ref[0, j, pl.ds(base, GRID_W), :] for j in range(slabs)], axis=1)
        zi = jnp.concatenate([z_ref[1, j, pl.ds(base, GRID_W), :] for j in range(slabs)], axis=1)
        zz = jnp.concatenate([zr, zi], axis=0).astype(BF16)
        x = jnp.dot(f2_ref[...], zz, preferred_element_type=F32)
        for j in range(slabs):
            sl = slice(j * LANES, (j + 1) * LANES)
            x2_ref[j, pl.ds(k1, GRID_W, stride=ROW_PITCH), :] = x[:GRID_W, sl]
            x2_ref[slabs + j, pl.ds(k1, GRID_W, stride=ROW_PITCH), :] = x[GRID_W:, sl]
        return carry

    lax.fori_loop(0, n1, stage2, 0, unroll=FFT_UNROLL)

    chunk = 8 * ROW_PITCH
    for c in range(GRID_W * ROW_PITCH // chunk):
        rs = slice(c * chunk, (c + 1) * chunk)
        xri = jnp.concatenate([x2_ref[s, rs, :] for s in range(2 * slabs)], axis=1).astype(BF16)
        o_ref[rs, :] = jnp.dot(xri, cs_ref[...], preferred_element_type=F32)


def _fft_lat(u_pad, batch):
    l = GRID_W * GRID_W
    rows_pad = GRID_W * ROW_PITCH
    slabs = FFT_CB // LANES
    c, s = _dft_cos_sin(GRID_W)
    f1 = _bf16_table(np.concatenate([c, -s], axis=0))
    f2 = _bf16_table(np.block([[c, s], [-s, c]]))
    n2k1 = (np.arange(GRID_W)[:, None] * np.arange(GRID_W)[None, :]) % l
    ang = 2.0 * np.pi * n2k1 / l
    twr = jnp.broadcast_to(jnp.asarray(np.cos(ang), F32)[:, :, None], (GRID_W, GRID_W, LANES))
    twi = jnp.broadcast_to(jnp.asarray(-np.sin(ang), F32)[:, :, None], (GRID_W, GRID_W, LANES))
    cs = _bf16_table(_channel_dft(FFT_CB, (l * FN_GROUP_DIM) ** -0.5))
    const2 = lambda b, cb: (0, 0)
    const3 = lambda b, cb: (0, 0, 0)
    return pl.pallas_call(
        _fft_lat_kernel,
        grid=(batch, FN_WIDTH // FFT_CB),
        in_specs=[
            pl.BlockSpec((None, slabs, rows_pad, LANES), lambda b, cb: (b, cb, 0, 0)),
            pl.BlockSpec(f1.shape, const2),
            pl.BlockSpec(twr.shape, const3),
            pl.BlockSpec(twi.shape, const3),
            pl.BlockSpec(f2.shape, const2),
            pl.BlockSpec(cs.shape, const2),
        ],
        out_specs=pl.BlockSpec((None, rows_pad, FFT_CB), lambda b, cb: (b, 0, cb)),
        out_shape=jax.ShapeDtypeStruct((batch, rows_pad, FN_WIDTH), F32),
        scratch_shapes=[
            pltpu.VMEM((2, slabs, rows_pad, LANES), F32),
            pltpu.VMEM((2 * slabs, rows_pad, LANES), F32),
        ],
        compiler_params=pltpu.CompilerParams(vmem_limit_bytes=VMEM_LIMIT),
        name="fourier_lat",
    )(u_pad, f1, twr, twi, f2, cs)


def _outproj_kernel(x_ref, a_ref, f_ref, m_ref, w_ref, o_ref, *, is_ctx):
    if is_ctx:
        f = f_ref[...]
    else:
        f = jnp.concatenate(
            [f_ref[r * ROW_PITCH:r * ROW_PITCH + GRID_W, :] for r in range(TM // GRID_W)], axis=0).astype(BF16)
    af = jnp.concatenate([a_ref[...], f], axis=1)
    mix = jnp.dot(af, w_ref[...], preferred_element_type=F32)
    o_ref[...] = x_ref[...] + _mod_chunk(m_ref, 2) * mix


def _outproj(x2, a, f, mods4, row_of_tile, w_out, is_ctx, batch):
    n = x2.shape[0]
    tile = lambda i: (i, 0)
    if is_ctx:
        fspec = pl.BlockSpec((TM, FN_WIDTH), tile)
    else:
        tiles_per_batch = n // batch // TM
        fspec = pl.BlockSpec((None, TM // GRID_W * ROW_PITCH, FN_WIDTH),
                             lambda i: (i // tiles_per_batch, i % tiles_per_batch, 0))
    return pl.pallas_call(
        functools.partial(_outproj_kernel, is_ctx=is_ctx),
        grid=(n // TM,),
        in_specs=[
            pl.BlockSpec((TM, D_MODEL), tile),
            pl.BlockSpec((TM, NA_WIDTH), tile),
            fspec,
            _mods_spec(0, row_of_tile),
            pl.BlockSpec((D_MODEL, D_MODEL), lambda i: (0, 0)),
        ],
        out_specs=pl.BlockSpec((TM, D_MODEL), tile),
        out_shape=jax.ShapeDtypeStruct((n, D_MODEL), F32),
        name="outproj_ctx" if is_ctx else "outproj_lat",
    )(x2, a, f, mods4, w_out)


def _sg_kernel(x_ref, m_ref, g_ref, win_ref, vg_ref, ws_ref, bs_ref, wout_ref, o_ref):
    x = x_ref[...]
    h = _rms_mod(x, g_ref[...], _mod_chunk(m_ref, 0), _mod_chunk(m_ref, 1)).astype(BF16)
    z = jax.nn.gelu(jnp.dot(h, win_ref[...], preferred_element_type=F32))
    gated = []
    for g in range(SG_GROUPS):
        gs = slice(g * SG_GROUP_DIM, (g + 1) * SG_GROUP_DIM)
        u = z[:, gs]
        v = z[:, SG_WIDTH + g * SG_GROUP_DIM:SG_WIDTH + (g + 1) * SG_GROUP_DIM]
        ms = jnp.mean(v * v, axis=-1, keepdims=True)
        vn = (v * lax.rsqrt(ms + EPS) * vg_ref[:, gs]).astype(BF16)
        w = ws_ref[g]
        s = jnp.concatenate(
            [jnp.dot(w, vn[c * CHUNK:(c + 1) * CHUNK], preferred_element_type=F32) for c in range(TM // CHUNK)],
            axis=0)
        bias = jnp.concatenate([bs_ref[:, gs]] * (TM // CHUNK), axis=0)
        gated.append((u * (s + bias)).astype(BF16))
    mix = jnp.dot(jnp.concatenate(gated, axis=1), wout_ref[...], preferred_element_type=F32)
    o_ref[...] = x + _mod_chunk(m_ref, 2) * mix


def _spatial_gating(x2, mods4, row_of_tile, g, w_in, v_gain, w_s, b_full, w_out):
    n = x2.shape[0]
    tile = lambda i: (i, 0)
    const = lambda i: (0, 0)
    return pl.pallas_call(
        _sg_kernel,
        grid=(n // TM,),
        in_specs=[
            pl.BlockSpec((TM, D_MODEL), tile),
            _mods_spec(1, row_of_tile),
            pl.BlockSpec((1, D_MODEL), const),
            pl.BlockSpec((D_MODEL, 2 * SG_WIDTH), const),
            pl.BlockSpec((1, SG_WIDTH), const),
            pl.BlockSpec((SG_GROUPS, CHUNK, CHUNK), lambda i: (0, 0, 0)),
            pl.BlockSpec((CHUNK, SG_WIDTH), const),
            pl.BlockSpec((SG_WIDTH, D_MODEL), const),
        ],
        out_specs=pl.BlockSpec((TM, D_MODEL), tile),
        out_shape=jax.ShapeDtypeStruct((n, D_MODEL), F32),
        name="spatial_gating",
    )(x2, mods4, g, w_in, v_gain, w_s, b_full, w_out)


def _first_index_of(mask, idx, size):
    return jnp.min(jnp.where(mask, idx, size), axis=0, keepdims=True)


def _route(h, h_hi, rw_ref, rb_ref):
    h_lo = (h - h_hi.astype(F32)).astype(BF16)
    rw = rw_ref[...]
    rw_hi = rw.astype(BF16)
    rw_lo = (rw - rw_hi.astype(F32)).astype(BF16)
    hi_both = jnp.dot(h_hi, jnp.concatenate([rw_hi, rw_lo], axis=1), preferred_element_type=F32)
    logits = (hi_both[:, :LANES] + jnp.dot(h_lo, rw_hi, preferred_element_type=F32)
              + hi_both[:, LANES:])
    logits = logits.T[:N_EXPERTS]
    scores = jax.nn.sigmoid(logits)
    sel = scores + rb_ref[...]
    eg = EXPERTS_PER_GROUP
    ri = lax.broadcasted_iota(jnp.int32, (eg, TM), 0)
    firsts, seconds, group_scores = [], [], []
    for g in range(N_EXPERT_GROUPS):
        a = sel[g * eg:(g + 1) * eg]
        m1 = a.max(axis=0, keepdims=True)
        i1 = _first_index_of(a == m1, ri, eg)
        rest = jnp.where(ri == i1, -jnp.inf, a)
        m2 = rest.max(axis=0, keepdims=True)
        i2 = _first_index_of(rest == m2, ri, eg)
        firsts.append(i1)
        seconds.append(i2)
        group_scores.append(m1 + m2)
    gs = jnp.concatenate(group_scores, axis=0)
    gi = lax.broadcasted_iota(jnp.int32, gs.shape, 0)
    g_best = _first_index_of(gs == gs.max(axis=0, keepdims=True), gi, N_EXPERT_GROUPS)
    picked = []
    for g in range(N_EXPERT_GROUPS):
        chosen = (g_best == g) & ((ri == firsts[g]) | (ri == seconds[g]))
        picked.append(jnp.where(chosen, scores[g * eg:(g + 1) * eg], 0.0))
    w = jnp.concatenate(picked, axis=0)
    return w / w.sum(axis=0, keepdims=True), g_best


def _split_bf16(t):
    hi = t.astype(BF16)
    return hi, (t - hi.astype(F32)).astype(BF16)


def _tn_dot(a, b):
    return lax.dot_general(a, b, (((0,), (0,)), ((), ())), preferred_element_type=F32)


def _moe_kernel(x_ref, m_ref, g_ref, rw_ref, rb_ref, tri_ref, wg_ref, wu_ref, wd_ref, o_ref):
    x = x_ref[...]
    h = _rms_mod(x, g_ref[...], _mod_chunk(m_ref, 3), _mod_chunk(m_ref, 4))
    h_hi = h.astype(BF16)
    w, g_best = _route(h, h_hi, rw_ref, rb_ref)
    wt = jnp.concatenate([w, jnp.zeros((LANES - N_EXPERTS, TM), F32)], axis=0)
    gates_hi, gates_lo = _split_bf16(wt.T)

    gi = lax.broadcasted_iota(jnp.int32, (2 * N_EXPERT_GROUPS, TM), 0)
    onehot = jnp.where(gi == g_best, 1.0, 0.0)
    prefix = jnp.dot(onehot.astype(BF16), tri_ref[...], preferred_element_type=F32)
    rank = jnp.sum(onehot * prefix, axis=0, keepdims=True)
    slot = lax.broadcasted_iota(jnp.int32, (MOE_CHUNK, TM), 0).astype(F32)

    o_ref[...] = x
    gate2 = _mod_chunk(m_ref, 5)
    eg = EXPERTS_PER_GROUP
    for g in range(N_EXPERT_GROUPS):
        count = jnp.sum(onehot[g:g + 1]).astype(jnp.int32)

        def chunk(j, carry, g=g):
            key = jnp.where(g_best == g, rank - (j * MOE_CHUNK).astype(F32), -1.0)
            perm = jnp.where(slot == key, 1.0, 0.0).astype(BF16)
            hs = jnp.dot(perm, h_hi, preferred_element_type=F32).astype(BF16)
            gs = (jnp.dot(perm, gates_hi, preferred_element_type=F32)
                  + jnp.dot(perm, gates_lo, preferred_element_type=F32))
            y = jnp.zeros((MOE_CHUNK, D_MODEL), F32)
            for e in range(g * eg, (g + 1) * eg):
                a = jnp.dot(hs, wg_ref[e], preferred_element_type=F32)
                b = jnp.dot(hs, wu_ref[e], preferred_element_type=F32)
                hid = jax.nn.silu(a) * b * gs[:, e:e + 1]
                y = y + jnp.dot(hid.astype(BF16), wd_ref[e], preferred_element_type=F32)
            o_ref[...] += _tn_dot(perm, (gate2 * y).astype(BF16))
            return carry

        lax.fori_loop(0, (count + MOE_CHUNK - 1) // MOE_CHUNK, chunk, 0)


def _moe(x2, mods4, layer, row_of_tile, g, rw, rb, wg, wu, wd):
    n = x2.shape[0]
    tile = lambda i: (i, 0)
    const = lambda i: (0, 0)
    resident = lambda shape: pl.BlockSpec(shape, lambda i: (0, 0, 0), pipeline_mode=pl.Buffered(1))
    tri = jnp.asarray(np.triu(np.ones((TM, TM), np.float32), k=1), BF16)
    return pl.pallas_call(
        _moe_kernel,
        grid=(n // TM,),
        in_specs=[
            pl.BlockSpec((TM, D_MODEL), tile),
            _mods_spec(layer, row_of_tile),
            pl.BlockSpec((1, D_MODEL), const),
            pl.BlockSpec((D_MODEL, LANES), const),
            pl.BlockSpec((N_EXPERTS, 1), const),
            pl.BlockSpec((TM, TM), const),
            resident(wg.shape), resident(wu.shape), resident(wd.shape),
        ],
        out_specs=pl.BlockSpec((TM, D_MODEL), tile),
        out_shape=jax.ShapeDtypeStruct((n, D_MODEL), F32),
        compiler_params=pltpu.CompilerParams(vmem_limit_bytes=VMEM_LIMIT),
        name="moe",
    )(x2, mods4, g, rw, rb, tri, wg, wu, wd)


def _trunk(x, mods4, row_of_tile, cache, p, is_ctx):
    batch, l, _ = x.shape
    x2 = x.reshape(batch * l, D_MODEL)

    outs = _inproj(x2, mods4, row_of_tile, p["norm1_g"][0], p["na_w_in"], p["q_gain"], p["k_gain"], p["head_bd"],
                   is_ctx, batch)
    if is_ctx:
        q, k, v, u, k32, v32 = outs
        a = _attn_ctx(q, k, v, batch)
        f = _fft_ctx(u, batch)
    else:
        q, k, v, u_pad = outs
        a = _attn_lat(q, k, v, cache[0], cache[1], p["bias_tbl"], batch)
        f = _fft_lat(u_pad, batch)
        k32 = v32 = None
    x2 = _outproj(x2, a, f, mods4, row_of_tile, p["na_w_out"], is_ctx, batch)
    x2 = _moe(x2, mods4, 0, row_of_tile, p["norm2_g"][0], p["router_w"], p["router_b"],
              p["moe_wg"][0], p["moe_wu"][0], p["moe_wd"][0])

    x2 = _spatial_gating(x2, mods4, row_of_tile, p["norm1_g"][1], p["sg_w_in"], p["sg_v_gain"], p["sg_w_s"],
                         p["sg_b_full"], p["sg_w_out"])
    x2 = _moe(x2, mods4, 1, row_of_tile, p["norm2_g"][1], p["router_w"], p["router_b"],
              p["moe_wg"][1], p["moe_wu"][1], p["moe_wd"][1])
    return x2.reshape(batch, l, D_MODEL), k32, v32


def kernel(x_prompt, x_sample, cache_k, cache_v, c, c_ctx, ada_w, ada_b, norm1_g, norm2_g, na_w_in, na_w_out,
           q_gain, k_gain, rpb, sg_w_in, sg_w_out, sg_v_gain, sg_w_s, sg_b_s, router_w, router_b,
           moe_w_gate, moe_w_up, moe_w_down):
    batch, seq, _ = x_prompt.shape
    dec_batch, dec_seq, _ = x_sample.shape
    assert dec_batch <= CTX_ROW and dec_seq == GRID_W * GRID_W and (batch * seq) % TM == 0
    assert ada_w.shape[0] == 2 and na_w_in.shape[0] == 1 and sg_w_in.shape[0] == 1

    cond = jnp.zeros((N_COND_ROWS, D_MODEL), F32).at[:dec_batch].set(c).at[CTX_ROW].set(c_ctx)
    mods = _ada_table(cond, ada_w, ada_b)
    mods4 = mods.reshape(mods.shape[0], N_COND_ROWS, 1, mods.shape[2])

    tile_rep = lambda t: jnp.tile(t, NA_HEADS).reshape(1, NA_WIDTH)
    p = {
        "norm1_g": norm1_g.reshape(-1, 1, D_MODEL),
        "norm2_g": norm2_g.reshape(-1, 1, D_MODEL),
        "na_w_in": na_w_in[0].astype(BF16),
        "na_w_out": na_w_out[0].astype(BF16),
        "q_gain": tile_rep(q_gain[0]),
        "k_gain": tile_rep(k_gain[0]),
        "head_bd": jnp.asarray(np.kron(np.eye(NA_HEADS), np.full((HEAD_DIM, HEAD_DIM), 1.0 / HEAD_DIM)), BF16),
        "bias_tbl": _bias_table(rpb[0], dec_seq // GRID_W),
        "sg_w_in": sg_w_in[0].astype(BF16),
        "sg_w_out": sg_w_out[0].astype(BF16),
        "sg_v_gain": sg_v_gain[0].reshape(1, SG_WIDTH),
        "sg_w_s": sg_w_s[0].astype(BF16),
        "sg_b_full": jnp.broadcast_to(sg_b_s[0].T[:, :, None], (CHUNK, SG_GROUPS, SG_GROUP_DIM)).reshape(
            CHUNK, SG_WIDTH),
        "router_w": jnp.pad(router_w, ((0, 0), (0, LANES - N_EXPERTS))),
        "router_b": router_b.reshape(N_EXPERTS, 1),
        "moe_wg": moe_w_gate.astype(BF16),
        "moe_wu": moe_w_up.astype(BF16),
        "moe_wd": moe_w_down.astype(BF16),
    }

    y_prompt, k32, v32 = _trunk(x_prompt, mods4, lambda i: CTX_ROW, None, p, True)
    k_ctx = k32.reshape(batch, 1, seq, NA_HEADS, HEAD_DIM)
    v_ctx = v32.reshape(batch, 1, seq, NA_HEADS, HEAD_DIM)

    tiles_per_batch = dec_seq // TM
    cache = (cache_k[:, 0].reshape(dec_batch, -1, NA_WIDTH).astype(BF16),
             cache_v[:, 0].reshape(dec_batch, -1, NA_WIDTH).astype(BF16))
    y_sample, _, _ = _trunk(x_sample, mods4, lambda i: i // tiles_per_batch, cache, p, False)
    return (y_prompt, y_sample, k_ctx, v_ctx)
```
